```python
import math
import jax, jax.numpy as jnp
from jax import lax
import numpy as np

D_MODEL = 2048
BATCH = 2
SEQ = 8192
DEPTH = 4

GRID_W = 64
CTX_LEN = 256
N_MIXERS = 2
SSM_EXPAND = 2
D_INNER = SSM_EXPAND * D_MODEL
SSM_HEAD_DIM = 64
SSM_HEADS = D_INNER // SSM_HEAD_DIM
SSM_GROUPS = 8
SSM_STATE = 128
SSM_CONV = 5
SSM_CHUNK = 128
XBC_DIM = D_INNER + 2 * SSM_GROUPS * SSM_STATE
SSM_IN_DIM = D_INNER + XBC_DIM + 2 * SSM_HEADS
CONF_KERNEL = 31
D_FF = 4 * D_MODEL
EPS = 1e-6

kernel_name = "hybrid_ssd_conformer_dit_trunk"


def rmsnorm(x, g):
    xf = x.astype(jnp.float32)
    y = xf * lax.rsqrt(jnp.mean(xf * xf, axis=-1, keepdims=True) + EPS)
    return (y * g.astype(jnp.float32)).astype(x.dtype)


def layernorm(x, g, b):
    xf = x.astype(jnp.float32)
    mu = jnp.mean(xf, axis=-1, keepdims=True)
    xc = xf - mu
    var = jnp.mean(xc * xc, axis=-1, keepdims=True)
    y = xc * lax.rsqrt(var + EPS) * g.astype(jnp.float32) + b.astype(jnp.float32)
    return y.astype(x.dtype)


def modulate(h, shift, scale):
    return h * (1 + scale) + shift


def dwconv(u, w, b, seg_len):
    bsz, l, ch = u.shape
    k = w.shape[0]
    us = u.reshape(bsz * (l // seg_len), seg_len, ch)
    y = lax.conv_general_dilated(
        us, w[:, None, :].astype(u.dtype), window_strides=(1,),
        padding=[(k // 2, k // 2)], dimension_numbers=("NWC", "WIO", "NWC"),
        feature_group_count=ch)
    return y.reshape(bsz, l, ch) + b.astype(u.dtype)


def to_scan_order(u, rows, col_major):
    if not col_major:
        return u
    b, l, d = u.shape
    return u.reshape(b, rows, GRID_W, d).swapaxes(1, 2).reshape(b, l, d)


def from_scan_order(u, rows, col_major):
    if not col_major:
        return u
    b, l, d = u.shape
    return u.reshape(b, GRID_W, rows, d).swapaxes(1, 2).reshape(b, l, d)


def ssd_scan(xh, dt, a, bm, cm, h0):
    b, l, h, p = xh.shape
    g, n = bm.shape[-2:]
    q = SSM_CHUNK
    nc = l // q
    hg = h // g
    f32 = jnp.float32
    x_c = xh.astype(f32).reshape(b, nc, q, g, hg, p)
    dt_c = dt.reshape(b, nc, q, g, hg)
    b_c = bm.astype(f32).reshape(b, nc, q, g, n)
    c_c = cm.astype(f32).reshape(b, nc, q, g, n)
    xdt = x_c * dt_c[..., None]
    cs = jnp.cumsum(dt_c * a.reshape(g, hg), axis=2)
    mask = jnp.tril(jnp.ones((q, q), dtype=bool))[None, None, :, :, None, None]
    seg = cs[:, :, :, None] - cs[:, :, None, :]
    decay = jnp.exp(jnp.where(mask, seg, -jnp.inf))
    scores = jnp.einsum("bclgn,bcsgn->bclsg", c_c, b_c)
    y_diag = jnp.einsum("bclsgk,bcsgkp->bclgkp", scores[..., None] * decay, xdt)
    decay_to_end = jnp.exp(cs[:, :, -1:] - cs)
    states = jnp.einsum("bcsgn,bcsgkp->bcgkpn", b_c, xdt * decay_to_end[..., None])
    chunk_decay = jnp.exp(cs[:, :, -1])

    def step(hc, inp):
        st, dec = inp
        return dec[..., None, None] * hc + st, hc

    h_final, h_in = lax.scan(
        step, h0.astype(f32).reshape(b, g, hg, p, n),
        (jnp.moveaxis(states, 1, 0), jnp.moveaxis(chunk_decay, 1, 0)))
    h_in = jnp.moveaxis(h_in, 0, 1)
    y_off = jnp.einsum("bclgn,bcgkpn->bclgkp", c_c, h_in) * jnp.exp(cs)[..., None]
    y = (y_diag + y_off).reshape(b, l, h, p)
    return y, h_final.reshape(b, h, p, n)


def ssd_direction(xs, dt_raw, bm, cm, a_log, dt_bias, d_skip, h0, reverse):
    if reverse:
        xs, dt_raw, bm, cm = (jnp.flip(t, axis=1) for t in (xs, dt_raw, bm, cm))
    dt = jax.nn.softplus(dt_raw.astype(jnp.float32) + dt_bias.astype(jnp.float32))
    a = -jnp.exp(a_log.astype(jnp.float32))
    y, hf = ssd_scan(xs, dt, a, bm, cm, h0)
    y = y + d_skip.astype(jnp.float32)[:, None] * xs.astype(jnp.float32)
    if reverse:
        y = jnp.flip(y, axis=1)
    return y, hf


def mamba_mix(u, w_in, conv_w, conv_b, a_log_f, dt_bias_f, d_f, a_log_b, dt_bias_b,
              d_b, norm_g, w_out, h0_f, h0_b, with_output=True):
    b, l, _ = u.shape
    proj = u @ w_in
    z = proj[..., :D_INNER]
    xbc = proj[..., D_INNER:D_INNER + XBC_DIM]
    dt_raw = proj[..., D_INNER + XBC_DIM:]
    xbc = jax.nn.silu(dwconv(xbc, conv_w, conv_b, l))
    gn = SSM_GROUPS * SSM_STATE
    xs = xbc[..., :D_INNER].reshape(b, l, SSM_HEADS, SSM_HEAD_DIM)
    bm = xbc[..., D_INNER:D_INNER + gn].reshape(b, l, SSM_GROUPS, SSM_STATE)
    cm = xbc[..., D_INNER + gn:].reshape(b, l, SSM_GROUPS, SSM_STATE)
    y_f, hf = ssd_direction(xs, dt_raw[..., :SSM_HEADS], bm, cm, a_log_f, dt_bias_f, d_f,
                            h0_f, reverse=False)
    y_b, hb = ssd_direction(xs, dt_raw[..., SSM_HEADS:], bm, cm, a_log_b, dt_bias_b, d_b,
                            h0_b, reverse=True)
    if not with_output:
        return None, hf, hb
    y = (y_f + y_b).reshape(b, l, D_INNER)
    y = rmsnorm(y * jax.nn.silu(z.astype(jnp.float32)), norm_g)
    return y.astype(u.dtype) @ w_out, hf, hb


def conformer_mix(u, pw1_w, pw1_b, dw_w, dw_b, ln_g, ln_b, pw2_w, pw2_b, seg_len):
    a = u @ pw1_w + pw1_b
    v = a[..., :D_MODEL] * jax.nn.sigmoid(a[..., D_MODEL:])
    v = dwconv(v, dw_w, dw_b, seg_len)
    v = layernorm(v, ln_g, ln_b)
    v = v * jax.nn.sigmoid(v)
    return v @ pw2_w + pw2_b


def sq_relu_mlp(u, w1, w2):
    return jnp.square(jax.nn.relu(u @ w1)) @ w2


def setup_inputs(seed: int = 0) -> dict:
    key = jax.random.key(seed)
    ks = jax.random.split(key, 40)
    f32 = jnp.float32
    n_ssm = (DEPTH + 1) // 2
    n_conf = DEPTH // 2

    def nrm(k, shape, scale):
        return jax.random.normal(k, shape, f32) * scale

    def gain(k, shape):
        return 1.0 + 0.05 * jax.random.normal(k, shape, f32)

    def dt_bias(k):
        dt = jnp.exp(jax.random.uniform(k, (n_ssm, SSM_HEADS), f32,
                                        math.log(1e-3), math.log(1e-1)))
        return dt + jnp.log(-jnp.expm1(-dt))

    return {
        "x": nrm(ks[0], (BATCH, SEQ, D_MODEL), 1.0),
        "c": nrm(ks[1], (BATCH, D_MODEL), 1.0),
        "ctx": nrm(ks[2], (BATCH, CTX_LEN, D_MODEL), 1.0),
        "c_ctx": nrm(ks[3], (D_MODEL,), 1.0),
        "mod_w": nrm(ks[4], (DEPTH, D_MODEL, 6 * D_MODEL), 0.5 * D_MODEL ** -0.5),
        "mod_b": nrm(ks[5], (DEPTH, 6 * D_MODEL), 0.02),
        "pre_mix_g": gain(ks[6], (DEPTH, D_MODEL)),
        "post_mix_g": gain(ks[7], (DEPTH, D_MODEL)),
        "pre_mlp_g": gain(ks[8], (DEPTH, D_MODEL)),
        "post_mlp_g": gain(ks[9], (DEPTH, D_MODEL)),
        "mlp_w1": nrm(ks[10], (DEPTH, D_MODEL, D_FF), D_MODEL ** -0.5),
        "mlp_w2": nrm(ks[11], (DEPTH, D_FF, D_MODEL), D_FF ** -0.5),
        "ssm_in_w": nrm(ks[12], (n_ssm, D_MODEL, SSM_IN_DIM), D_MODEL ** -0.5),
        "ssm_conv_w": nrm(ks[13], (n_ssm, SSM_CONV, XBC_DIM), SSM_CONV ** -0.5),
        "ssm_conv_b": nrm(ks[14], (n_ssm, XBC_DIM), 0.02),
        "ssm_a_log_f": jnp.log(jax.random.uniform(ks[15], (n_ssm, SSM_HEADS), f32, 1.0, 16.0)),
        "ssm_dt_bias_f": dt_bias(ks[16]),
        "ssm_d_f": gain(ks[17], (n_ssm, SSM_HEADS)),
        "ssm_a_log_b": jnp.log(jax.random.uniform(ks[18], (n_ssm, SSM_HEADS), f32, 1.0, 16.0)),
        "ssm_dt_bias_b": dt_bias(ks[19]),
        "ssm_d_b": gain(ks[20], (n_ssm, SSM_HEADS)),
        "ssm_norm_g": gain(ks[21], (n_ssm, D_INNER)),
        "ssm_out_w": nrm(ks[22], (n_ssm, D_INNER, D_MODEL), D_INNER ** -0.5),
        "conf_pw1_w": nrm(ks[23], (n_conf, D_MODEL, 2 * D_MODEL), D_MODEL ** -0.5),
        "conf_pw1_b": nrm(ks[24], (n_conf, 2 * D_MODEL), 0.02),
        "conf_dw_w": nrm(ks[25], (n_conf, CONF_KERNEL, D_MODEL), CONF_KERNEL ** -0.5),
        "conf_dw_b": nrm(ks[26], (n_conf, D_MODEL), 0.02),
        "conf_ln_g": gain(ks[27], (n_conf, D_MODEL)),
        "conf_ln_b": nrm(ks[28], (n_conf, D_MODEL), 0.02),
        "conf_pw2_w": nrm(ks[29], (n_conf, D_MODEL, D_MODEL), D_MODEL ** -0.5),
        "conf_pw2_b": nrm(ks[30], (n_conf, D_MODEL), 0.02),
    }


def reference(x, c, ctx, c_ctx, mod_w, mod_b, pre_mix_g, post_mix_g, pre_mlp_g, post_mlp_g,
              mlp_w1, mlp_w2, ssm_in_w, ssm_conv_w, ssm_conv_b, ssm_a_log_f, ssm_dt_bias_f,
              ssm_d_f, ssm_a_log_b, ssm_dt_bias_b, ssm_d_b, ssm_norm_g, ssm_out_w,
              conf_pw1_w, conf_pw1_b, conf_dw_w, conf_dw_b, conf_ln_g, conf_ln_b,
              conf_pw2_w, conf_pw2_b):
    bsz, seq_len, _ = x.shape
    rows = seq_len // GRID_W
    sc = jax.nn.silu(c)
    scc = jax.nn.silu(c_ctx)
    h, hc = x, ctx
    for i in range(DEPTH):
        last = i == DEPTH - 1
        kind = i % N_MIXERS
        j = i // N_MIXERS
        col_major = (j % 2) == 1
        m = sc @ mod_w[i] + mod_b[i]
        sh1, sc1, g1, sh2, sc2, g2 = jnp.split(m[:, None, :], 6, axis=-1)
        mc = scc @ mod_w[i] + mod_b[i]
        csh1, csc1, cg1, csh2, csc2, cg2 = jnp.split(mc, 6)

        u = modulate(rmsnorm(h, pre_mix_g[i]), sh1, sc1)
        uc = modulate(rmsnorm(hc, pre_mix_g[i]), csh1, csc1)
        u = to_scan_order(u, rows, col_major)
        if kind == 0:
            p = (ssm_in_w[j], ssm_conv_w[j], ssm_conv_b[j], ssm_a_log_f[j], ssm_dt_bias_f[j],
                 ssm_d_f[j], ssm_a_log_b[j], ssm_dt_bias_b[j], ssm_d_b[j], ssm_norm_g[j],
                 ssm_out_w[j])
            zeros = jnp.zeros((bsz, SSM_HEADS, SSM_HEAD_DIM, SSM_STATE), jnp.float32)
            yc, s_f, s_b = mamba_mix(uc, *p, zeros, zeros, with_output=not last)
            y, _, _ = mamba_mix(u, *p, s_f, s_b)
        else:
            p = (conf_pw1_w[j], conf_pw1_b[j], conf_dw_w[j], conf_dw_b[j], conf_ln_g[j],
                 conf_ln_b[j], conf_pw2_w[j], conf_pw2_b[j])
            seg = rows if col_major else GRID_W
            y = conformer_mix(u, *p, seg)
            yc = None if last else conformer_mix(uc, *p, uc.shape[1])
        y = from_scan_order(y, rows, col_major)
        h = h + (g1 * rmsnorm(y, post_mix_g[i])).astype(h.dtype)

        f = sq_relu_mlp(modulate(rmsnorm(h, pre_mlp_g[i]), sh2, sc2), mlp_w1[i], mlp_w2[i])
        h = h + (g2 * rmsnorm(f, post_mlp_g[i])).astype(h.dtype)

        if not last:
            hc = hc + (cg1 * rmsnorm(yc, post_mix_g[i])).astype(hc.dtype)
            fc = sq_relu_mlp(modulate(rmsnorm(hc, pre_mlp_g[i]), csh2, csc2),
                             mlp_w1[i], mlp_w2[i])
            hc = hc + (cg2 * rmsnorm(fc, post_mlp_g[i])).astype(hc.dtype)
    return h
```

```python
import functools

import jax
import jax.numpy as jnp
from jax import lax
from jax.experimental import pallas as pl
from jax.experimental.pallas import tpu as pltpu

F32 = jnp.float32
BF16 = jnp.bfloat16

EPS = 1e-6
GRID_W = 64
SSM_HEADS = 64
SSM_HEAD_DIM = 64
SSM_GROUPS = 8
SSM_STATE = 128
SSD_CHUNK = 128
D_INNER = SSM_HEADS * SSM_HEAD_DIM
GN = SSM_GROUPS * SSM_STATE
XBC = D_INNER + 2 * GN
ZX = D_INNER + XBC

V7X_VMEM_LIMIT = 56 * 1024 * 1024
BF16_ROWS = 16


def _cp(*sem):
    return pltpu.CompilerParams(dimension_semantics=sem, vmem_limit_bytes=V7X_VMEM_LIMIT)


def _dot(a, b):
    return jnp.dot(a, b, preferred_element_type=F32)


def _sigmoid(v):
    return 1.0 / (1.0 + jnp.exp(-v))


def _softplus(v):
    return jnp.maximum(v, 0.0) + jnp.log(1.0 + jnp.exp(-jnp.abs(v)))


def _tok_view(a, col_major, tm):
    bsz, l, d = a.shape
    if not col_major:
        return a, (None, tm, d), (lambda b, i: (b, i, 0)), tm, 1
    rows = l // GRID_W
    kc = tm // rows
    return (a.reshape(bsz, rows, GRID_W * d), (None, rows, kc * d),
            (lambda b, i: (b, 0, i)), rows, kc)


def _norm_mod_rows(x_ref, g_ref, sh_ref, sc_ref, u_ref, tr, kc, d):
    mul = g_ref[...] * (1.0 + sc_ref[...])
    sh = sh_ref[...]

    def body(i, carry):
        r0 = pl.multiple_of(i * BF16_ROWS, BF16_ROWS)
        for j in range(kc):
            xs = x_ref[pl.ds(r0, BF16_ROWS), j * d:(j + 1) * d]
            ms = jnp.mean(xs * xs, axis=-1, keepdims=True)
            u = xs * lax.rsqrt(ms + EPS) * mul + sh
            u_ref[pl.ds(j * tr + r0, BF16_ROWS), :] = u.astype(BF16)
        return carry

    lax.fori_loop(0, tr // BF16_ROWS, body, 0)


def _post_rows(acc_ref, b_ref, pg_ref, gate_ref, h_ref, o_ref, tr, kc, d):
    bias = b_ref[...]
    mul = pg_ref[...] * gate_ref[...]

    def body(i, carry):
        r0 = pl.multiple_of(i * 8, 8)
        for j in range(kc):
            f = acc_ref[pl.ds(j * tr + r0, 8), :] + bias
            ms = jnp.mean(f * f, axis=-1, keepdims=True)
            o_ref[pl.ds(r0, 8), j * d:(j + 1) * d] = (
                h_ref[pl.ds(r0, 8), j * d:(j + 1) * d] + f * lax.rsqrt(ms + EPS) * mul)
        return carry

    lax.fori_loop(0, tr // 8, body, 0)


def _adaln_kernel(c_ref, w_ref, b_ref, o_ref):
    cv = c_ref[...]
    s = cv * _sigmoid(cv)
    o_ref[0] = _dot(s.astype(BF16), w_ref[0].astype(BF16)) + b_ref[0]


def _adaln(c, c_ctx, mod_w, mod_b):
    depth, d, n = mod_w.shape
    bsz = c.shape[0]
    cc = jnp.zeros((8, d), F32).at[:bsz].set(c).at[bsz].set(c_ctx)
    tn = 1024
    out = pl.pallas_call(
        _adaln_kernel,
        out_shape=jax.ShapeDtypeStruct((depth, 8, n), F32),
        grid=(depth, n // tn),
        in_specs=[pl.BlockSpec((8, d), lambda i, j: (0, 0)),
                  pl.BlockSpec((1, d, tn), lambda i, j: (i, 0, j)),
                  pl.BlockSpec((1, 1, tn), lambda i, j: (i, 0, j))],
        out_specs=pl.BlockSpec((1, 8, tn), lambda i, j: (i, 0, j)),
        compiler_params=_cp("parallel", "parallel"),
        name="adaln",
    )(cc, mod_w, mod_b.reshape(depth, 1, n))
    return out.reshape(depth, 8, 6, d)


def _nmm_plain_kernel(x_ref, g_ref, sh_ref, sc_ref, w_ref, ws_ref, o_ref, os_ref, u_ref,
                      *, tr, kc, d):
    @pl.when(pl.program_id(2) == 0)
    def _():
        _norm_mod_rows(x_ref, g_ref, sh_ref, sc_ref, u_ref, tr, kc, d)
        os_ref[...] = _dot(u_ref[...], ws_ref[...])

    o_ref[...] = _dot(u_ref[...], w_ref[...]).astype(o_ref.dtype)


def _nmm_plain(x, g, sh, sc, w, w_side, col_major, tm, tn):
    bsz, l, d = x.shape
    n, ns = w.shape[1], w_side.shape[1]
    xv, xblk, xmap, tr, kc = _tok_view(x, col_major, tm)
    vec = pl.BlockSpec((1, d), lambda b, i, j: (0, 0))
    mod = pl.BlockSpec((None, 1, d), lambda b, i, j: (b, 0, 0))
    return pl.pallas_call(
        functools.partial(_nmm_plain_kernel, tr=tr, kc=kc, d=d),
        out_shape=(jax.ShapeDtypeStruct((bsz, l, n), BF16),
                   jax.ShapeDtypeStruct((bsz, l, ns), F32)),
        grid=(bsz, l // tm, n // tn),
        in_specs=[pl.BlockSpec(xblk, lambda b, i, j: xmap(b, i)), vec, mod, mod,
                  pl.BlockSpec((d, tn), lambda b, i, j: (0, j)),
                  pl.BlockSpec((d, ns), lambda b, i, j: (0, 0))],
        out_specs=(pl.BlockSpec((None, tm, tn), lambda b, i, j: (b, i, j)),
                   pl.BlockSpec((None, tm, ns), lambda b, i, j: (b, i, 0))),
        scratch_shapes=[pltpu.VMEM((tm, d), BF16)],
        compiler_params=_cp("parallel", "parallel", "arbitrary"),
        name="norm_mod_matmul",
    )(xv, g.reshape(1, d), sh, sc, w, w_side)


def _nmm_glu_kernel(x_ref, g_ref, sh_ref, sc_ref, wa_ref, wg_ref, ba_ref, bg_ref, o_ref, u_ref,
                    *, tr, kc, d):
    @pl.when(pl.program_id(2) == 0)
    def _():
        _norm_mod_rows(x_ref, g_ref, sh_ref, sc_ref, u_ref, tr, kc, d)

    a = _dot(u_ref[...], wa_ref[...]) + ba_ref[...]
    gt = _dot(u_ref[...], wg_ref[...]) + bg_ref[...]
    o_ref[...] = (a * _sigmoid(gt)).astype(o_ref.dtype)


def _nmm_glu(x, g, sh, sc, w, bias, col_major, tm, tn):
    bsz, l, d = x.shape
    n = w.shape[1] // 2
    nj = n // tn
    xv, xblk, xmap, tr, kc = _tok_view(x, col_major, tm)
    vec = pl.BlockSpec((1, d), lambda b, i, j: (0, 0))
    mod = pl.BlockSpec((None, 1, d), lambda b, i, j: (b, 0, 0))
    b2 = bias.reshape(1, 2 * n)
    return pl.pallas_call(
        functools.partial(_nmm_glu_kernel, tr=tr, kc=kc, d=d),
        out_shape=jax.ShapeDtypeStruct((bsz, l, n), BF16),
        grid=(bsz, l // tm, nj),
        in_specs=[pl.BlockSpec(xblk, lambda b, i, j: xmap(b, i)), vec, mod, mod,
                  pl.BlockSpec((d, tn), lambda b, i, j: (0, j)),
                  pl.BlockSpec((d, tn), lambda b, i, j: (0, j + nj)),
                  pl.BlockSpec((1, tn), lambda b, i, j: (0, j)),
                  pl.BlockSpec((1, tn), lambda b, i, j: (0, j + nj))],
        out_specs=pl.BlockSpec((None, tm, tn), lambda b, i, j: (b, i, j)),
        scratch_shapes=[pltpu.VMEM((tm, d), BF16)],
        compiler_params=_cp("parallel", "parallel", "arbitrary"),
        name="norm_mod_glu",
    )(xv, g.reshape(1, d), sh, sc, w, w, b2, b2)


def _post_kernel(y_ref, w_ref, b_ref, pg_ref, gate_ref, h_ref, o_ref, acc_ref, *, tr, kc, d):
    k = pl.program_id(2)
    part = _dot(y_ref[...], w_ref[...])

    @pl.when(k == 0)
    def _():
        acc_ref[...] = part

    @pl.when(k > 0)
    def _():
        acc_ref[...] += part

    @pl.when(k == pl.num_programs(2) - 1)
    def _():
        _post_rows(acc_ref, b_ref, pg_ref, gate_ref, h_ref, o_ref, tr, kc, d)


def _post(y, w, bias, pg, gate, h, col_major, tm, tk):
    bsz, l, d = h.shape
    kdim = y.shape[2]
    hv, hblk, hmap, tr, kc = _tok_view(h, col_major, tm)
    vec = pl.BlockSpec((1, d), lambda b, i, k: (0, 0))
    mod = pl.BlockSpec((None, 1, d), lambda b, i, k: (b, 0, 0))
    hspec = pl.BlockSpec(hblk, lambda b, i, k: hmap(b, i))
    out = pl.pallas_call(
        functools.partial(_post_kernel, tr=tr, kc=kc, d=d),
        out_shape=jax.ShapeDtypeStruct(hv.shape, F32),
        grid=(bsz, l // tm, kdim // tk),
        in_specs=[pl.BlockSpec((None, tm, tk), lambda b, i, k: (b, i, k)),
                  pl.BlockSpec((tk, d), lambda b, i, k: (k, 0)),
                  vec, vec, mod, hspec],
        out_specs=hspec,
        scratch_shapes=[pltpu.VMEM((tm, d), F32)],
        compiler_params=_cp("parallel", "parallel", "arbitrary"),
        name="matmul_post_norm_residual",
    )(y, w, bias.reshape(1, d), pg.reshape(1, d), gate, hv)
    return out.reshape(bsz, l, d)


def _mlp_kernel(x_ref, g_ref, sh_ref, sc_ref, w1_ref, w2_ref, pg_ref, gate_ref, z_ref,
                o_ref, u_ref, acc_ref, *, tm, d):
    k = pl.program_id(2)

    @pl.when(k == 0)
    def _():
        _norm_mod_rows(x_ref, g_ref, sh_ref, sc_ref, u_ref, tm, 1, d)

    hid = jnp.square(jnp.maximum(_dot(u_ref[...], w1_ref[...]), 0.0)).astype(BF16)
    part = _dot(hid, w2_ref[...])

    @pl.when(k == 0)
    def _():
        acc_ref[...] = part

    @pl.when(k > 0)
    def _():
        acc_ref[...] += part

    @pl.when(k == pl.num_programs(2) - 1)
    def _():
        _post_rows(acc_ref, z_ref, pg_ref, gate_ref, x_ref, o_ref, tm, 1, d)


def _mlp(h, g, sh, sc, w1, w2, pg, gate, tm, tf):
    bsz, l, d = h.shape
    ff = w1.shape[1]
    vec = pl.BlockSpec((1, d), lambda b, i, k: (0, 0))
    mod = pl.BlockSpec((None, 1, d), lambda b, i, k: (b, 0, 0))
    hspec = pl.BlockSpec((None, tm, d), lambda b, i, k: (b, i, 0))
    return pl.pallas_call(
        functools.partial(_mlp_kernel, tm=tm, d=d),
        out_shape=jax.ShapeDtypeStruct(h.shape, F32),
        grid=(bsz, l // tm, ff // tf),
        in_specs=[hspec, vec, mod, mod,
                  pl.BlockSpec((d, tf), lambda b, i, k: (0, k)),
                  pl.BlockSpec((tf, d), lambda b, i, k: (k, 0)),
                  vec, mod, vec],
        out_specs=hspec,
        scratch_shapes=[pltpu.VMEM((tm, d), BF16), pltpu.VMEM((tm, d), F32)],
        compiler_params=_cp("parallel", "parallel", "arbitrary"),
        name="mlp_sq_relu",
    )(h, g.reshape(1, d), sh, sc, w1, w2, pg.reshape(1, d), gate, jnp.zeros((1, d), F32))


def _ssm_conv_kernel(cur_ref, prev_ref, next_ref, w_ref, b_ref, o_ref, ext_ref, *, tl, kw):
    i = pl.program_id(1)
    halo = BF16_ROWS
    keep_prev = jnp.where(i > 0, 1.0, 0.0)
    keep_next = jnp.where(i < pl.num_programs(1) - 1, 1.0, 0.0)
    ext_ref[0:halo, :] = prev_ref[...].astype(F32) * keep_prev
    ext_ref[halo:halo + tl, :] = cur_ref[...].astype(F32)
    ext_ref[halo + tl:, :] = next_ref[...].astype(F32) * keep_next
    bias = b_ref[...]
    rc = 64
    for r in range(0, tl, rc):
        acc = bias + w_ref[0:1, :] * ext_ref[halo - kw // 2 + r: halo - kw // 2 + r + rc, :]
        for k in range(1, kw):
            s = halo - kw // 2 + k + r
            acc = acc + w_ref[k:k + 1, :] * ext_ref[s:s + rc, :]
        o_ref[r:r + rc, :] = (acc * _sigmoid(acc)).astype(o_ref.dtype)


def _ssm_conv(zx, conv_w, conv_b, tl, tc):
    bsz, l, _ = zx.shape
    kw, ch = conv_w.shape
    off = D_INNER // tc
    hb = tl // BF16_ROWS
    nhb = l // BF16_ROWS
    return pl.pallas_call(
        functools.partial(_ssm_conv_kernel, tl=tl, kw=kw),
        out_shape=jax.ShapeDtypeStruct((bsz, l, ch), BF16),
        grid=(bsz, l // tl, ch // tc),
        in_specs=[pl.BlockSpec((None, tl, tc), lambda b, i, j: (b, i, j + off)),
                  pl.BlockSpec((None, BF16_ROWS, tc),
                               lambda b, i, j: (b, jnp.maximum(i * hb - 1, 0), j + off)),
                  pl.BlockSpec((None, BF16_ROWS, tc),
                               lambda b, i, j: (b, jnp.minimum((i + 1) * hb, nhb - 1), j + off)),
                  pl.BlockSpec((kw, tc), lambda b, i, j: (0, j)),
                  pl.BlockSpec((1, tc), lambda b, i, j: (0, j))],
        out_specs=pl.BlockSpec((None, tl, tc), lambda b, i, j: (b, i, j)),
        scratch_shapes=[pltpu.VMEM((tl + 2 * BF16_ROWS, tc), F32)],
        compiler_params=_cp("parallel", "parallel", "parallel"),
        name="ssm_conv_silu",
    )(zx, zx, zx, conv_w, conv_b.reshape(1, ch))


def _split2(v):
    hi = v.astype(BF16)
    return hi, (v - hi.astype(F32)).astype(BF16)


def _ssd_kernel(*refs, reverse, gated):
    if gated:
        (xbc_ref, dt_ref, alog_ref, dtb_ref, dsk_ref, e2_ref, h0_ref, yf_ref, z_ref, ng_ref,
         y_ref, hf_ref, s_ref, ybuf_ref) = refs
    else:
        (xbc_ref, dt_ref, alog_ref, dtb_ref, dsk_ref, e2_ref, h0_ref,
         y_ref, hf_ref, s_ref) = refs
        ybuf_ref = y_ref
    c = pl.program_id(1)
    q = SSD_CHUNK
    hp = SSM_HEAD_DIM
    gw = (SSM_HEADS // SSM_GROUPS) * hp
    lane0 = SSM_HEADS if reverse else 0
    end = 0 if reverse else q - 1

    @pl.when(c == 0)
    def _():
        s_ref[...] = h0_ref[...]

    dt = _softplus(dt_ref[...] + dtb_ref[...])
    dta = dt * (-jnp.exp(alog_ref[...]))
    ri = lax.broadcasted_iota(jnp.int32, (q, q), 0)
    ci = lax.broadcasted_iota(jnp.int32, (q, q), 1)
    tri = (ci >= ri) if reverse else (ci <= ri)
    tmat = jnp.where(tri, 1.0, 0.0).astype(BF16)
    p0, r0 = dta.astype(BF16), None
    r0 = dta - p0.astype(F32)
    p1 = r0.astype(BF16)
    p2 = (r0 - p1.astype(F32)).astype(BF16)
    cs = _dot(tmat, p0) + _dot(tmat, p1) + _dot(tmat, p2)
    cs_t = cs.T
    cs_end = cs[end:end + 1, :]
    e2 = e2_ref[...]

    def expand(v):
        hi, lo = _split2(v)
        return _dot(jnp.concatenate([hi, lo], axis=1), e2)

    dt_x = expand(dt)
    ecs_x = expand(jnp.exp(cs))
    dte_x = expand(jnp.exp(cs_end - cs))
    lane = lax.broadcasted_iota(jnp.int32, (q, 2 * hp), 1)

    for g in range(SSM_GROUPS):
        lo, hi = g * gw, (g + 1) * gw
        bg = xbc_ref[:, D_INNER + g * SSM_STATE:D_INNER + (g + 1) * SSM_STATE]
        cg = xbc_ref[:, D_INNER + GN + g * SSM_STATE:D_INNER + GN + (g + 1) * SSM_STATE]
        scores = lax.dot_general(cg, bg, (((1,), (1,)), ((), ())), preferred_element_type=F32)
        xs_g = xbc_ref[:, lo:hi].astype(F32)
        xdt_g = xs_g * dt_x[:, lo:hi]
        xdt_b = xdt_g.astype(BF16)
        s_g = s_ref[:, lo:hi]
        y_g = _dot(cg, s_g.astype(BF16)) * ecs_x[:, lo:hi] + dsk_ref[:, lo:hi] * xs_g
        s_ref[:, lo:hi] = ecs_x[end:end + 1, lo:hi] * s_g + lax.dot_general(
            bg, (xdt_g * dte_x[:, lo:hi]).astype(BF16), (((0,), (0,)), ((), ())),
            preferred_element_type=F32)
        for kp in range(gw // (2 * hp)):
            gmats = []
            for hh in range(2):
                hd = lane0 + g * (gw // hp) + 2 * kp + hh
                seg = cs[:, hd:hd + 1] - cs_t[hd:hd + 1, :]
                decay = jnp.exp(jnp.where(tri, seg, -jnp.inf))
                gmats.append((scores * decay).astype(BF16))
            xp = xdt_b[:, kp * 2 * hp:(kp + 1) * 2 * hp]
            rhs = jnp.concatenate([jnp.where(lane < hp, xp, jnp.zeros_like(xp)),
                                   jnp.where(lane >= hp, xp, jnp.zeros_like(xp))], axis=0)
            yd = _dot(jnp.concatenate(gmats, axis=1), rhs)
            ybuf_ref[:, lo + kp * 2 * hp:lo + (kp + 1) * 2 * hp] = (
                yd + y_g[:, kp * 2 * hp:(kp + 1) * 2 * hp])

    @pl.when(c == pl.num_programs(1) - 1)
    def _():
        hf_ref[...] = s_ref[...]

    if gated:
        ng = ng_ref[...]

        def body(i, carry):
            r = pl.multiple_of(i * BF16_ROWS, BF16_ROWS)
            zz = z_ref[pl.ds(r, BF16_ROWS), :].astype(F32)
            v = (ybuf_ref[pl.ds(r, BF16_ROWS), :] + yf_ref[pl.ds(r, BF16_ROWS), :]) * (
                zz * _sigmoid(zz))
            ms = jnp.mean(v * v, axis=-1, keepdims=True)
            y_ref[pl.ds(r, BF16_ROWS), :] = (v * lax.rsqrt(ms + EPS) * ng).astype(BF16)
            return carry

        lax.fori_loop(0, q // BF16_ROWS, body, 0)


def _ssd(xa, dt, a_log, dt_bias, d_skip, e2, h0, reverse, gate_args=None):
    bsz, l, _ = xa.shape
    nc = l // SSD_CHUNK
    gated = gate_args is not None
    cmap = (lambda b, c: (b, nc - 1 - c, 0)) if reverse else (lambda b, c: (b, c, 0))
    const = lambda b, c: (0, 0)
    state = pl.BlockSpec((None, SSM_STATE, D_INNER), lambda b, c: (b, 0, 0))
    tok = pl.BlockSpec((None, SSD_CHUNK, D_INNER), cmap)
    in_specs = [pl.BlockSpec((None, SSD_CHUNK, XBC), cmap),
                pl.BlockSpec((None, SSD_CHUNK, 2 * SSM_HEADS), cmap),
                pl.BlockSpec((1, 2 * SSM_HEADS), const),
                pl.BlockSpec((1, 2 * SSM_HEADS), const),
                pl.BlockSpec((1, D_INNER), const),
                pl.BlockSpec((4 * SSM_HEADS, D_INNER), const),
                state]
    args = [xa, dt, a_log, dt_bias, d_skip, e2, h0]
    scratch = [pltpu.VMEM((SSM_STATE, D_INNER), F32)]
    if gated:
        y_fwd, zx, norm_g = gate_args
        in_specs += [tok, tok, pl.BlockSpec((1, D_INNER), const)]
        args += [y_fwd, zx, norm_g.reshape(1, D_INNER)]
        scratch.append(pltpu.VMEM((SSD_CHUNK, D_INNER), F32))
    return pl.pallas_call(
        functools.partial(_ssd_kernel, reverse=reverse, gated=gated),
        out_shape=(jax.ShapeDtypeStruct((bsz, l, D_INNER), BF16 if gated else F32),
                   jax.ShapeDtypeStruct((bsz, SSM_STATE, D_INNER), F32)),
        grid=(bsz, nc),
        in_specs=in_specs,
        out_specs=(tok, state),
        scratch_shapes=scratch,
        compiler_params=_cp("parallel", "arbitrary"),
        name="ssd_bwd_gate_norm" if gated else "ssd_fwd",
    )(*args)


def _conf_conv_kernel(v_ref, w_ref, b_ref, lg_ref, lb_ref, o_ref, ext_ref, cv_ref,
                      *, tl, seg, kw, d):
    pad = BF16_ROWS
    stride = seg + 2 * pad
    nseg = tl // seg
    for s in range(nseg):
        base = s * stride
        ext_ref[base:base + pad, :] = jnp.zeros((pad, d), F32)
        ext_ref[base + pad:base + pad + seg, :] = v_ref[s * seg:(s + 1) * seg, :].astype(F32)
        ext_ref[base + pad + seg:base + stride, :] = jnp.zeros((pad, d), F32)
    rc, lc = 64, 256
    for s in range(nseg):
        for r in range(0, seg, rc):
            for l0 in range(0, d, lc):
                start = s * stride + pad - kw // 2 + r
                acc = b_ref[:, l0:l0 + lc] + w_ref[0:1, l0:l0 + lc] * ext_ref[start:start + rc,
                                                                          l0:l0 + lc]
                for k in range(1, kw):
                    acc = acc + w_ref[k:k + 1, l0:l0 + lc] * ext_ref[start + k:start + k + rc,
                                                                     l0:l0 + lc]
                cv_ref[s * seg + r:s * seg + r + rc, l0:l0 + lc] = acc
    lg = lg_ref[...]
    lb = lb_ref[...]

    def body(i, carry):
        r = pl.multiple_of(i * BF16_ROWS, BF16_ROWS)
        xv = cv_ref[pl.ds(r, BF16_ROWS), :]
        xc = xv - jnp.mean(xv, axis=-1, keepdims=True)
        var = jnp.mean(xc * xc, axis=-1, keepdims=True)
        y = xc * lax.rsqrt(var + EPS) * lg + lb
        o_ref[pl.ds(r, BF16_ROWS), :] = (y * _sigmoid(y)).astype(BF16)
        return carry

    lax.fori_loop(0, tl // BF16_ROWS, body, 0)


def _conf_conv(v, dw_w, dw_b, ln_g, ln_b, seg, tl):
    bsz, l, d = v.shape
    kw = dw_w.shape[0]
    vec = pl.BlockSpec((1, d), lambda b, i: (0, 0))
    tok = pl.BlockSpec((None, tl, d), lambda b, i: (b, i, 0))
    return pl.pallas_call(
        functools.partial(_conf_conv_kernel, tl=tl, seg=seg, kw=kw, d=d),
        out_shape=jax.ShapeDtypeStruct((bsz, l, d), BF16),
        grid=(bsz, l // tl),
        in_specs=[tok, pl.BlockSpec((kw, d), lambda b, i: (0, 0)), vec, vec, vec],
        out_specs=tok,
        scratch_shapes=[pltpu.VMEM(((tl // seg) * (seg + 2 * BF16_ROWS), d), F32),
                        pltpu.VMEM((tl, d), F32)],
        compiler_params=_cp("parallel", "parallel"),
        name="conf_conv_ln_swish",
    )(v, dw_w, dw_b.reshape(1, d), ln_g.reshape(1, d), ln_b.reshape(1, d))


def _ssm_stream(h, mods, pre_g, post_g, w_zx, w_dt, conv_w, conv_b, ssd_f, ssd_b, e2f, e2b,
                norm_g, w_out, h0f, h0b, col_major, tm):
    sh1, sc1, g1 = mods
    d = h.shape[2]
    zx, dt = _nmm_plain(h, pre_g, sh1, sc1, w_zx, w_dt, col_major, tm, 1024)
    xa = _ssm_conv(zx, conv_w, conv_b, tm, 512)
    y_f, s_f = _ssd(xa, dt, *ssd_f, e2f, h0f, reverse=False)
    y, s_b = _ssd(xa, dt, *ssd_b, e2b, h0b, reverse=True, gate_args=(y_f, zx, norm_g))
    h = _post(y, w_out, jnp.zeros((d,), F32), post_g, g1, h, col_major, tm, 2048)
    return h, s_f, s_b


def _conf_stream(h, mods, pre_g, post_g, pw1_w, pw1_b, dw_w, dw_b, ln_g, ln_b, pw2_w, pw2_b,
                 seg, col_major, tm):
    sh1, sc1, g1 = mods
    v = _nmm_glu(h, pre_g, sh1, sc1, pw1_w, pw1_b, col_major, tm, 1024)
    u = _conf_conv(v, dw_w, dw_b, ln_g, ln_b, seg, 256)
    return _post(u, pw2_w, pw2_b, post_g, g1, h, col_major, tm, pw2_w.shape[0])


def kernel(x, c, ctx, c_ctx, mod_w, mod_b, pre_mix_g, post_mix_g, pre_mlp_g, post_mlp_g, mlp_w1, mlp_w2, ssm_in_w, ssm_conv_w, ssm_conv_b, ssm_a_log_f, ssm_dt_bias_f, ssm_d_f, ssm_a_log_b, ssm_dt_bias_b, ssm_d_b, ssm_norm_g, ssm_out_w, conf_pw1_w, conf_pw1_b, conf_dw_w, conf_dw_b, conf_ln_g, conf_ln_b, conf_pw2_w, conf_pw2_b):
    bsz, seq_len, d = x.shape
    ctx_len = ctx.shape[1]
    depth = mod_w.shape[0]
    rows = seq_len // GRID_W
    tm, tmc = 512, ctx_len
    mods = _adaln(c, c_ctx, mod_w, mod_b)

    head_of_chan = jnp.arange(D_INNER, dtype=jnp.int32) // SSM_HEAD_DIM
    lane_head = jnp.arange(2 * SSM_HEADS, dtype=jnp.int32)
    e_f = (lane_head[:, None] == head_of_chan[None, :]).astype(BF16)
    e_b = (lane_head[:, None] == head_of_chan[None, :] + SSM_HEADS).astype(BF16)
    e2f = jnp.concatenate([e_f, e_f], axis=0)
    e2b = jnp.concatenate([e_b, e_b], axis=0)

    h, hc = x, ctx
    for i in range(depth):
        last = i == depth - 1
        kind = i % 2
        j = i // 2
        col_major = (j % 2) == 1
        lat = [mods[i, :bsz, k][:, None, :] for k in range(6)]
        cm = [jnp.broadcast_to(mods[i, bsz, k][None, None, :], (bsz, 1, d)) for k in range(6)]
        if kind == 0:
            w_in = ssm_in_w[j].astype(BF16)
            w_zx, w_dt = w_in[:, :ZX], w_in[:, ZX:]
            a_log = jnp.concatenate([ssm_a_log_f[j], ssm_a_log_b[j]]).reshape(1, -1)
            dt_bias = jnp.concatenate([ssm_dt_bias_f[j], ssm_dt_bias_b[j]]).reshape(1, -1)
            ssd_f = (a_log, dt_bias, jnp.repeat(ssm_d_f[j], SSM_HEAD_DIM).reshape(1, -1))
            ssd_b = (a_log, dt_bias, jnp.repeat(ssm_d_b[j], SSM_HEAD_DIM).reshape(1, -1))
            common = (w_zx, w_dt, ssm_conv_w[j], ssm_conv_b[j], ssd_f, ssd_b, e2f, e2b,
                      ssm_norm_g[j], ssm_out_w[j].astype(BF16))
            zeros = jnp.zeros((bsz, SSM_STATE, D_INNER), F32)
            hc_new, s_f, s_b = _ssm_stream(hc, cm[:3], pre_mix_g[i], post_mix_g[i], *common,
                                           zeros, zeros, False, tmc)
            h, _, _ = _ssm_stream(h, lat[:3], pre_mix_g[i], post_mix_g[i], *common,
                                  s_f, s_b, col_major, tm)
        else:
            common = (conf_pw1_w[j].astype(BF16), conf_pw1_b[j], conf_dw_w[j], conf_dw_b[j],
                      conf_ln_g[j], conf_ln_b[j], conf_pw2_w[j].astype(BF16), conf_pw2_b[j])
            seg = rows if col_major else GRID_W
            if not last:
                hc_new = _conf_stream(hc, cm[:3], pre_mix_g[i], post_mix_g[i], *common,
                                      ctx_len, False, tmc)
            h = _conf_stream(h, lat[:3], pre_mix_g[i], post_mix_g[i], *common,
                             seg, col_major, tm)
        w1 = mlp_w1[i].astype(BF16)
        w2 = mlp_w2[i].astype(BF16)
        h = _mlp(h, pre_mlp_g[i], lat[3], lat[4], w1, w2, post_mlp_g[i], lat[5], tm, 1024)
        if not last:
            hc = _mlp(hc_new, pre_mlp_g[i], cm[3], cm[4], w1, w2, post_mlp_g[i], cm[5], tmc, 1024)
    return h
```

```python
import functools

import jax
import jax.numpy as jnp
from jax import lax
from jax.experimental import pallas as pl
from jax.experimental.pallas import tpu as pltpu

F32 = jnp.float32
BF16 = jnp.bfloat16

EPS = 1e-6
GRID_W = 64
SSM_HEADS = 64
SSM_HEAD_DIM = 64
SSM_GROUPS = 8
SSM_STATE = 128
SSD_CHUNK = 128
D_INNER = SSM_HEADS * SSM_HEAD_DIM
GN = SSM_GROUPS * SSM_STATE
XBC = D_INNER + 2 * GN
ZX = D_INNER + XBC

V7X_VMEM_LIMIT = 56 * 1024 * 1024
F32_ROWS = 8
BF16_ROWS = 16
ROW_GROUPS = 4


def _cp(*sem):
    return pltpu.CompilerParams(dimension_semantics=sem, vmem_limit_bytes=V7X_VMEM_LIMIT)


def _dot(a, b):
    return jnp.dot(a, b, preferred_element_type=F32)


def _sigmoid(v):
    return 1.0 / (1.0 + jnp.exp(-v))


def _softplus(v):
    return jnp.maximum(v, 0.0) + jnp.log(1.0 + jnp.exp(-jnp.abs(v)))


def _row_access(ref, transposed, seg):
    if transposed:
        def load(r, n):
            return ref[pl.ds(r, n), seg, :]

        def store(r, v):
            ref[pl.ds(r, v.shape[0]), seg, :] = v
    else:
        def load(r, n):
            return ref[pl.ds(seg * GRID_W + r, n), :]

        def store(r, v):
            ref[pl.ds(seg * GRID_W + r, v.shape[0]), :] = v
    return load, store


def _norm_mod_rows(load, u_ref, u_row0, ntok, mul, sh):
    step = F32_ROWS * ROW_GROUPS

    def body(i, carry):
        r0 = pl.multiple_of(i * step, step)
        rs = []
        for s in range(ROW_GROUPS):
            xs = load(r0 + F32_ROWS * s, F32_ROWS)
            rs.append(lax.rsqrt(jnp.mean(xs * xs, axis=-1, keepdims=True) + EPS))
        for s in range(0, ROW_GROUPS, 2):
            ua = load(r0 + F32_ROWS * s, F32_ROWS) * rs[s] * mul + sh
            ub = load(r0 + F32_ROWS * (s + 1), F32_ROWS) * rs[s + 1] * mul + sh
            u_ref[pl.ds(u_row0 + r0 + F32_ROWS * s, BF16_ROWS), :] = (
                jnp.concatenate([ua, ub], axis=0).astype(BF16))
        return carry

    lax.fori_loop(0, ntok // step, body, 0)


def _post_rows(acc_ref, acc_row0, ntok, bias, mul, load_h, store_o):
    step = F32_ROWS * ROW_GROUPS

    def body(i, carry):
        r0 = pl.multiple_of(i * step, step)
        rs = []
        for s in range(ROW_GROUPS):
            f = acc_ref[pl.ds(acc_row0 + r0 + F32_ROWS * s, F32_ROWS), :] + bias
            rs.append(lax.rsqrt(jnp.mean(f * f, axis=-1, keepdims=True) + EPS))
        for s in range(ROW_GROUPS):
            f = acc_ref[pl.ds(acc_row0 + r0 + F32_ROWS * s, F32_ROWS), :] + bias
            store_o(r0 + F32_ROWS * s, load_h(r0 + F32_ROWS * s, F32_ROWS) + f * rs[s] * mul)
        return carry

    lax.fori_loop(0, ntok // step, body, 0)


def _adaln_kernel(c_ref, w_ref, b_ref, o_ref):
    cv = c_ref[...]
    s = cv * _sigmoid(cv)
    o_ref[0] = _dot(s.astype(BF16), w_ref[0].astype(BF16)) + b_ref[0]


def _adaln(c, c_ctx, mod_w, mod_b):
    depth, d, n = mod_w.shape
    bsz = c.shape[0]
    cc = jnp.zeros((8, d), F32).at[:bsz].set(c).at[bsz].set(c_ctx)
    tn = 1024
    out = pl.pallas_call(
        _adaln_kernel,
        out_shape=jax.ShapeDtypeStruct((depth, 8, n), F32),
        grid=(depth, n // tn),
        in_specs=[pl.BlockSpec((8, d), lambda i, j: (0, 0)),
                  pl.BlockSpec((1, d, tn), lambda i, j: (i, 0, j)),
                  pl.BlockSpec((1, 1, tn), lambda i, j: (i, 0, j))],
        out_specs=pl.BlockSpec((1, 8, tn), lambda i, j: (i, 0, j)),
        compiler_params=_cp("parallel", "parallel"),
        name="adaln",
    )(cc, mod_w, mod_b.reshape(depth, 1, n))
    return out.reshape(depth, 8, 6, d)


def _nmm_plain_kernel(x_ref, g_ref, sh_ref, sc_ref, w_ref, ws_ref, o_ref, os_ref, u_ref, *, tm):
    @pl.when(pl.program_id(2) == 0)
    def _():
        load, _ = _row_access(x_ref, False, 0)
        _norm_mod_rows(load, u_ref, 0, tm, g_ref[...] * (1.0 + sc_ref[...]), sh_ref[...])
        os_ref[...] = _dot(u_ref[...], ws_ref[...])

    o_ref[...] = _dot(u_ref[...], w_ref[...]).astype(o_ref.dtype)


def _nmm_plain(x, g, sh, sc, w, w_side, tm, tn):
    bsz, l, d = x.shape
    n, ns = w.shape[1], w_side.shape[1]
    vec = pl.BlockSpec((1, d), lambda b, i, j: (0, 0))
    mod = pl.BlockSpec((None, 1, d), lambda b, i, j: (b, 0, 0))
    return pl.pallas_call(
        functools.partial(_nmm_plain_kernel, tm=tm),
        out_shape=(jax.ShapeDtypeStruct((bsz, l, n), BF16),
                   jax.ShapeDtypeStruct((bsz, l, ns), F32)),
        grid=(bsz, l // tm, n // tn),
        in_specs=[pl.BlockSpec((None, tm, d), lambda b, i, j: (b, i, 0)), vec, mod, mod,
                  pl.BlockSpec((d, tn), lambda b, i, j: (0, j)),
                  pl.BlockSpec((d, ns), lambda b, i, j: (0, 0))],
        out_specs=(pl.BlockSpec((None, tm, tn), lambda b, i, j: (b, i, j)),
                   pl.BlockSpec((None, tm, ns), lambda b, i, j: (b, i, 0))),
        scratch_shapes=[pltpu.VMEM((tm, d), BF16)],
        compiler_params=_cp("parallel", "parallel", "arbitrary"),
        name="norm_mod_matmul",
    )(x, g.reshape(1, d), sh, sc, w, w_side)


def _nmm_glu_kernel(x_ref, g_ref, sh_ref, sc_ref, wa_ref, wg_ref, ba_ref, bg_ref, o_ref, u_ref,
                    *, tm):
    @pl.when(pl.program_id(2) == 0)
    def _():
        load, _ = _row_access(x_ref, False, 0)
        _norm_mod_rows(load, u_ref, 0, tm, g_ref[...] * (1.0 + sc_ref[...]), sh_ref[...])

    a = _dot(u_ref[...], wa_ref[...]) + ba_ref[...]
    gt = _dot(u_ref[...], wg_ref[...]) + bg_ref[...]
    o_ref[...] = (a * _sigmoid(gt)).astype(o_ref.dtype)


def _nmm_glu(x, g, sh, sc, w, bias, tm, tn):
    bsz, l, d = x.shape
    n = w.shape[1] // 2
    nj = n // tn
    vec = pl.BlockSpec((1, d), lambda b, i, j: (0, 0))
    mod = pl.BlockSpec((None, 1, d), lambda b, i, j: (b, 0, 0))
    b2 = bias.reshape(1, 2 * n)
    return pl.pallas_call(
        functools.partial(_nmm_glu_kernel, tm=tm),
        out_shape=jax.ShapeDtypeStruct((bsz, l, n), BF16),
        grid=(bsz, l // tm, nj),
        in_specs=[pl.BlockSpec((None, tm, d), lambda b, i, j: (b, i, 0)), vec, mod, mod,
                  pl.BlockSpec((d, tn), lambda b, i, j: (0, j)),
                  pl.BlockSpec((d, tn), lambda b, i, j: (0, j + nj)),
                  pl.BlockSpec((1, tn), lambda b, i, j: (0, j)),
                  pl.BlockSpec((1, tn), lambda b, i, j: (0, j + nj))],
        out_specs=pl.BlockSpec((None, tm, tn), lambda b, i, j: (b, i, j)),
        scratch_shapes=[pltpu.VMEM((tm, d), BF16)],
        compiler_params=_cp("parallel", "parallel", "arbitrary"),
        name="norm_mod_glu",
    )(x, g.reshape(1, d), sh, sc, w, w, b2, b2)


def _post_kernel(y_ref, w_ref, b_ref, pg_ref, gate_ref, h_ref, o_ref, acc_ref, *, tm):
    k = pl.program_id(2)
    part = _dot(y_ref[...], w_ref[...])

    @pl.when(k == 0)
    def _():
        acc_ref[...] = part

    @pl.when(k > 0)
    def _():
        acc_ref[...] += part

    @pl.when(k == pl.num_programs(2) - 1)
    def _():
        load_h, _ = _row_access(h_ref, False, 0)
        _, store_o = _row_access(o_ref, False, 0)
        _post_rows(acc_ref, 0, tm, b_ref[...], pg_ref[...] * gate_ref[...], load_h, store_o)


def _post(y, w, bias, pg, gate, h, tm, tk):
    bsz, l, d = h.shape
    kdim = y.shape[2]
    vec = pl.BlockSpec((1, d), lambda b, i, k: (0, 0))
    mod = pl.BlockSpec((None, 1, d), lambda b, i, k: (b, 0, 0))
    hspec = pl.BlockSpec((None, tm, d), lambda b, i, k: (b, i, 0))
    return pl.pallas_call(
        functools.partial(_post_kernel, tm=tm),
        out_shape=jax.ShapeDtypeStruct(h.shape, F32),
        grid=(bsz, l // tm, kdim // tk),
        in_specs=[pl.BlockSpec((None, tm, tk), lambda b, i, k: (b, i, k)),
                  pl.BlockSpec((tk, d), lambda b, i, k: (k, 0)),
                  vec, vec, mod, hspec],
        out_specs=hspec,
        scratch_shapes=[pltpu.VMEM((tm, d), F32)],
        compiler_params=_cp("parallel", "parallel", "arbitrary"),
        name="matmul_post_norm_residual",
    )(y, w, bias.reshape(1, d), pg.reshape(1, d), gate, h)


def _mlp_kernel(x_ref, g_ref, sh_ref, sc_ref, w1_ref, w2_ref, pg_ref, gate_ref,
                o_ref, u_ref, acc_ref, *, tm, in_t, out_t):
    k = pl.program_id(2)
    if in_t or out_t:
        segs = [(s, GRID_W) for s in range(tm // GRID_W)]
    else:
        segs = [(0, tm)]

    @pl.when(k == 0)
    def _():
        mul = g_ref[...] * (1.0 + sc_ref[...])
        sh = sh_ref[...]
        for s, n in segs:
            load, _ = _row_access(x_ref, in_t, s)
            _norm_mod_rows(load, u_ref, s * GRID_W, n, mul, sh)

    hid = jnp.square(jnp.maximum(_dot(u_ref[...], w1_ref[...]), 0.0)).astype(BF16)
    part = _dot(hid, w2_ref[...])

    @pl.when(k == 0)
    def _():
        acc_ref[...] = part

    @pl.when(k > 0)
    def _():
        acc_ref[...] += part

    @pl.when(k == pl.num_programs(2) - 1)
    def _():
        mul = pg_ref[...] * gate_ref[...]
        for s, n in segs:
            load_h, _ = _row_access(x_ref, in_t, s)
            _, store_o = _row_access(o_ref, out_t, s)
            _post_rows(acc_ref, s * GRID_W, n, 0.0, mul, load_h, store_o)


def _mlp(h, g, sh, sc, w1, w2, pg, gate, tm, tf, in_t=False, out_t=False):
    bsz, l, d = h.shape
    ff = w1.shape[1]
    rows = l // GRID_W
    vec = pl.BlockSpec((1, d), lambda b, i, k: (0, 0))
    mod = pl.BlockSpec((None, 1, d), lambda b, i, k: (b, 0, 0))
    plain = pl.BlockSpec((None, tm, d), lambda b, i, k: (b, i, 0))
    if in_t or out_t:
        assert tm % GRID_W == 0 and (tm // GRID_W) % F32_ROWS == 0 and not (in_t and out_t)
    grid_view = pl.BlockSpec((None, GRID_W, tm // GRID_W, d), lambda b, i, k: (b, 0, i, 0))
    xin = h.reshape(bsz, GRID_W, rows, d) if in_t else h
    oshape = (bsz, GRID_W, rows, d) if out_t else (bsz, l, d)
    out = pl.pallas_call(
        functools.partial(_mlp_kernel, tm=tm, in_t=in_t, out_t=out_t),
        out_shape=jax.ShapeDtypeStruct(oshape, F32),
        grid=(bsz, l // tm, ff // tf),
        in_specs=[grid_view if in_t else plain, vec, mod, mod,
                  pl.BlockSpec((d, tf), lambda b, i, k: (0, k)),
                  pl.BlockSpec((tf, d), lambda b, i, k: (k, 0)),
                  vec, mod],
        out_specs=grid_view if out_t else plain,
        scratch_shapes=[pltpu.VMEM((tm, d), BF16), pltpu.VMEM((tm, d), F32)],
        compiler_params=_cp("parallel", "parallel", "arbitrary"),
        name="mlp_sq_relu",
    )(xin, g.reshape(1, d), sh, sc, w1, w2, pg.reshape(1, d), gate)
    return out.reshape(bsz, l, d)


def _shift_matrix(tb, kw, ext, pad, seg):
    t = jnp.arange(tb, dtype=jnp.int32)
    k = jnp.arange(kw, dtype=jnp.int32)
    e = jnp.arange(ext, dtype=jnp.int32) - pad
    src = t[:, None, None] + k[None, :, None] - kw // 2
    hit = src == e[None, None, :]
    if seg is not None:
        hit = hit & (src // seg == t[:, None, None] // seg)
    return hit.reshape(tb, kw * ext).astype(BF16)


def _scale_rows_by_taps(x, w_ref, xw_ref, row0, ext, kw):
    n, ch = x.shape
    x3 = x.reshape(n // F32_ROWS, F32_ROWS, ch)
    for k in range(kw):
        xw_ref[k * ext + row0:k * ext + row0 + n, :] = (
            (x3 * w_ref[k][None]).reshape(n, ch).astype(BF16))


def _ssm_conv_kernel(cur_ref, prev_ref, next_ref, s_ref, w_ref, b_ref, o_ref, xw_ref,
                     *, tl, tb, kw):
    i = pl.program_id(1)
    halo = BF16_ROWS
    ext = tb + 2 * halo
    rc = 32
    nblk = tl // tb
    keep_prev = jnp.where(i > 0, 1.0, 0.0)
    keep_next = jnp.where(i < pl.num_programs(1) - 1, 1.0, 0.0)
    for blk in range(nblk):
        t0 = blk * tb
        if blk == 0:
            before = prev_ref[...].astype(F32) * keep_prev
        else:
            before = cur_ref[t0 - halo:t0, :].astype(F32)
        if blk == nblk - 1:
            after = next_ref[...].astype(F32) * keep_next
        else:
            after = cur_ref[t0 + tb:t0 + tb + halo, :].astype(F32)
        _scale_rows_by_taps(before, w_ref, xw_ref, 0, ext, kw)
        for r in range(0, tb, rc):
            _scale_rows_by_taps(cur_ref[t0 + r:t0 + r + rc, :].astype(F32), w_ref, xw_ref,
                                halo + r, ext, kw)
        _scale_rows_by_taps(after, w_ref, xw_ref, halo + tb, ext, kw)
        acc = _dot(s_ref[...], xw_ref[...]) + b_ref[...]
        o_ref[t0:t0 + tb, :] = (acc * _sigmoid(acc)).astype(o_ref.dtype)


def _ssm_conv(zx, conv_w, conv_b, tl, tc):
    bsz, l, _ = zx.shape
    kw, ch = conv_w.shape
    tb = 128
    ext = tb + 2 * BF16_ROWS
    off = D_INNER // tc
    hb = tl // BF16_ROWS
    nhb = l // BF16_ROWS
    return pl.pallas_call(
        functools.partial(_ssm_conv_kernel, tl=tl, tb=tb, kw=kw),
        out_shape=jax.ShapeDtypeStruct((bsz, l, ch), BF16),
        grid=(bsz, l // tl, ch // tc),
        in_specs=[pl.BlockSpec((None, tl, tc), lambda b, i, j: (b, i, j + off)),
                  pl.BlockSpec((None, BF16_ROWS, tc),
                               lambda b, i, j: (b, jnp.maximum(i * hb - 1, 0), j + off)),
                  pl.BlockSpec((None, BF16_ROWS, tc),
                               lambda b, i, j: (b, jnp.minimum((i + 1) * hb, nhb - 1), j + off)),
                  pl.BlockSpec((tb, kw * ext), lambda b, i, j: (0, 0)),
                  pl.BlockSpec((kw, F32_ROWS, tc), lambda b, i, j: (0, 0, j)),
                  pl.BlockSpec((1, tc), lambda b, i, j: (0, j))],
        out_specs=pl.BlockSpec((None, tl, tc), lambda b, i, j: (b, i, j)),
        scratch_shapes=[pltpu.VMEM((kw * ext, tc), BF16)],
        compiler_params=_cp("parallel", "parallel", "parallel"),
        name="ssm_conv_silu",
    )(zx, zx, zx, _shift_matrix(tb, kw, ext, BF16_ROWS, None),
      jnp.broadcast_to(conv_w[:, None, :], (kw, F32_ROWS, ch)), conv_b.reshape(1, ch))


def _split2(v):
    hi = v.astype(BF16)
    return hi, (v - hi.astype(F32)).astype(BF16)


def _ssd_kernel(*refs, reverse, gated):
    if gated:
        (xbc_ref, dt_ref, alog_ref, dtb_ref, dsk_ref, e2_ref, h0_ref, yf_ref, z_ref, ng_ref,
         y_ref, hf_ref, s_ref, ybuf_ref) = refs
    else:
        (xbc_ref, dt_ref, alog_ref, dtb_ref, dsk_ref, e2_ref, h0_ref,
         y_ref, hf_ref, s_ref) = refs
        ybuf_ref = y_ref
    c = pl.program_id(1)
    q = SSD_CHUNK
    hp = SSM_HEAD_DIM
    gw = (SSM_HEADS // SSM_GROUPS) * hp
    lane0 = SSM_HEADS if reverse else 0
    end = 0 if reverse else q - 1

    @pl.when(c == 0)
    def _():
        s_ref[...] = h0_ref[...]

    dt = _softplus(dt_ref[...] + dtb_ref[...])
    dta = dt * (-jnp.exp(alog_ref[...]))
    ri = lax.broadcasted_iota(jnp.int32, (q, q), 0)
    ci = lax.broadcasted_iota(jnp.int32, (q, q), 1)
    tri = (ci >= ri) if reverse else (ci <= ri)
    tmat = jnp.where(tri, 1.0, 0.0).astype(BF16)
    p0 = dta.astype(BF16)
    r0 = dta - p0.astype(F32)
    p1 = r0.astype(BF16)
    p2 = (r0 - p1.astype(F32)).astype(BF16)
    cs = _dot(tmat, p0) + _dot(tmat, p1) + _dot(tmat, p2)
    cs_t = cs.T
    cs_end = cs[end:end + 1, :]
    e2 = e2_ref[...]

    def expand(v):
        hi, lo = _split2(v)
        return _dot(jnp.concatenate([hi, lo], axis=1), e2)

    dt_x = expand(dt)
    ecs_x = expand(jnp.exp(cs))
    dte_x = expand(jnp.exp(cs_end - cs))
    lane = lax.broadcasted_iota(jnp.int32, (q, 2 * hp), 1)

    for g in range(SSM_GROUPS):
        lo, hi = g * gw, (g + 1) * gw
        bg = xbc_ref[:, D_INNER + g * SSM_STATE:D_INNER + (g + 1) * SSM_STATE]
        cg = xbc_ref[:, D_INNER + GN + g * SSM_STATE:D_INNER + GN + (g + 1) * SSM_STATE]
        scores = lax.dot_general(cg, bg, (((1,), (1,)), ((), ())), preferred_element_type=F32)
        xs_g = xbc_ref[:, lo:hi].astype(F32)
        xdt_g = xs_g * dt_x[:, lo:hi]
        xdt_b = xdt_g.astype(BF16)
        s_g = s_ref[:, lo:hi]
        y_g = _dot(cg, s_g.astype(BF16)) * ecs_x[:, lo:hi] + dsk_ref[:, lo:hi] * xs_g
        s_ref[:, lo:hi] = ecs_x[end:end + 1, lo:hi] * s_g + lax.dot_general(
            bg, (xdt_g * dte_x[:, lo:hi]).astype(BF16), (((0,), (0,)), ((), ())),
            preferred_element_type=F32)
        for kp in range(gw // (2 * hp)):
            gmats = []
            for hh in range(2):
                hd = lane0 + g * (gw // hp) + 2 * kp + hh
                seg = cs[:, hd:hd + 1] - cs_t[hd:hd + 1, :]
                decay = jnp.exp(jnp.where(tri, seg, -jnp.inf))
                gmats.append((scores * decay).astype(BF16))
            xp = xdt_b[:, kp * 2 * hp:(kp + 1) * 2 * hp]
            rhs = jnp.concatenate([jnp.where(lane < hp, xp, jnp.zeros_like(xp)),
                                   jnp.where(lane >= hp, xp, jnp.zeros_like(xp))], axis=0)
            yd = _dot(jnp.concatenate(gmats, axis=1), rhs)
            cols = slice(lo + kp * 2 * hp, lo + (kp + 1) * 2 * hp)
            y_pair = yd + y_g[:, kp * 2 * hp:(kp + 1) * 2 * hp]
            if gated:
                zz = z_ref[:, cols].astype(F32)
                y_pair = (y_pair + yf_ref[:, cols]) * (zz * _sigmoid(zz))
            ybuf_ref[:, cols] = y_pair

    @pl.when(c == pl.num_programs(1) - 1)
    def _():
        hf_ref[...] = s_ref[...]

    if gated:
        load, _ = _row_access(ybuf_ref, False, 0)
        _norm_mod_rows(load, y_ref, 0, q, ng_ref[...], 0.0)


def _ssd(xa, dt, a_log, dt_bias, d_skip, e2, h0, reverse, gate_args=None):
    bsz, l, _ = xa.shape
    nc = l // SSD_CHUNK
    gated = gate_args is not None
    cmap = (lambda b, c: (b, nc - 1 - c, 0)) if reverse else (lambda b, c: (b, c, 0))
    const = lambda b, c: (0, 0)
    state = pl.BlockSpec((None, SSM_STATE, D_INNER), lambda b, c: (b, 0, 0))
    tok = pl.BlockSpec((None, SSD_CHUNK, D_INNER), cmap)
    in_specs = [pl.BlockSpec((None, SSD_CHUNK, XBC), cmap),
                pl.BlockSpec((None, SSD_CHUNK, 2 * SSM_HEADS), cmap),
                pl.BlockSpec((1, 2 * SSM_HEADS), const),
                pl.BlockSpec((1, 2 * SSM_HEADS), const),
                pl.BlockSpec((1, D_INNER), const),
                pl.BlockSpec((4 * SSM_HEADS, D_INNER), const),
                state]
    args = [xa, dt, a_log, dt_bias, d_skip, e2, h0]
    scratch = [pltpu.VMEM((SSM_STATE, D_INNER), F32)]
    if gated:
        y_fwd, zx, norm_g = gate_args
        in_specs += [tok, tok, pl.BlockSpec((1, D_INNER), const)]
        args += [y_fwd, zx, norm_g.reshape(1, D_INNER)]
        scratch.append(pltpu.VMEM((SSD_CHUNK, D_INNER), F32))
    return pl.pallas_call(
        functools.partial(_ssd_kernel, reverse=reverse, gated=gated),
        out_shape=(jax.ShapeDtypeStruct((bsz, l, D_INNER), BF16 if gated else F32),
                   jax.ShapeDtypeStruct((bsz, SSM_STATE, D_INNER), F32)),
        grid=(bsz, nc),
        in_specs=in_specs,
        out_specs=(tok, state),
        scratch_shapes=scratch,
        compiler_params=_cp("parallel", "arbitrary"),
        name="ssd_bwd_gate_norm" if gated else "ssd_fwd",
    )(*args)


def _conf_conv_kernel(v_ref, s_ref, w_ref, b_ref, lg_ref, lb_ref, o_ref, xw_ref, cv_ref,
                      *, tl, tb, kw, lc):
    j = pl.program_id(2)
    rc = 32
    for blk in range(tl // tb):
        for r in range(0, tb, rc):
            _scale_rows_by_taps(v_ref[blk * tb + r:blk * tb + r + rc, :].astype(F32),
                                w_ref, xw_ref, r, tb, kw)
        cv_ref[j, blk * tb:(blk + 1) * tb, :] = _dot(s_ref[...], xw_ref[...]) + b_ref[...]

    @pl.when(j == pl.num_programs(2) - 1)
    def _():
        nj = cv_ref.shape[0]
        inv_d = 1.0 / (nj * lc)
        step = F32_ROWS * ROW_GROUPS

        def body(i, carry):
            r0 = pl.multiple_of(i * step, step)
            stats = []
            for s in range(ROW_GROUPS):
                rows = pl.ds(r0 + F32_ROWS * s, F32_ROWS)
                tot = cv_ref[0, rows, :]
                for c in range(1, nj):
                    tot = tot + cv_ref[c, rows, :]
                mu = jnp.sum(tot, axis=-1, keepdims=True) * inv_d
                sq = (cv_ref[0, rows, :] - mu) * (cv_ref[0, rows, :] - mu)
                for c in range(1, nj):
                    sq = sq + (cv_ref[c, rows, :] - mu) * (cv_ref[c, rows, :] - mu)
                stats.append((mu, lax.rsqrt(jnp.sum(sq, axis=-1, keepdims=True) * inv_d + EPS)))
            for s in range(0, ROW_GROUPS, 2):
                for c in range(nj):
                    halves = []
                    for t in (s, s + 1):
                        mu, rstd = stats[t]
                        xc = cv_ref[c, pl.ds(r0 + F32_ROWS * t, F32_ROWS), :] - mu
                        halves.append(xc * rstd * lg_ref[:, c * lc:(c + 1) * lc]
                                      + lb_ref[:, c * lc:(c + 1) * lc])
                    y = jnp.concatenate(halves, axis=0)
                    o_ref[pl.ds(r0 + F32_ROWS * s, BF16_ROWS), c * lc:(c + 1) * lc] = (
                        y * _sigmoid(y)).astype(BF16)
            return carry

        lax.fori_loop(0, tl // step, body, 0)


def _conf_conv(v, dw_w, dw_b, ln_g, ln_b, seg, tl, tb, lc):
    bsz, l, d = v.shape
    kw = dw_w.shape[0]
    vec = pl.BlockSpec((1, d), lambda b, i, j: (0, 0))
    return pl.pallas_call(
        functools.partial(_conf_conv_kernel, tl=tl, tb=tb, kw=kw, lc=lc),
        out_shape=jax.ShapeDtypeStruct((bsz, l, d), BF16),
        grid=(bsz, l // tl, d // lc),
        in_specs=[pl.BlockSpec((None, tl, lc), lambda b, i, j: (b, i, j)),
                  pl.BlockSpec((tb, kw * tb), lambda b, i, j: (0, 0)),
                  pl.BlockSpec((kw, F32_ROWS, lc), lambda b, i, j: (0, 0, j)),
                  pl.BlockSpec((1, lc), lambda b, i, j: (0, j)),
                  vec, vec],
        out_specs=pl.BlockSpec((None, tl, d), lambda b, i, j: (b, i, 0)),
        scratch_shapes=[pltpu.VMEM((kw * tb, lc), BF16), pltpu.VMEM((d // lc, tl, lc), F32)],
        compiler_params=_cp("parallel", "parallel", "arbitrary"),
        name="conf_conv_ln_swish",
    )(v, _shift_matrix(tb, kw, tb, 0, seg),
      jnp.broadcast_to(dw_w[:, None, :], (kw, F32_ROWS, d)),
      dw_b.reshape(1, d), ln_g.reshape(1, d), ln_b.reshape(1, d))


def _ssm_stream(h, mods, pre_g, post_g, w_zx, w_dt, conv_w, conv_b, ssd_f, ssd_b, e2f, e2b,
                norm_g, w_out, h0f, h0b, tm):
    sh1, sc1, g1 = mods
    d = h.shape[2]
    zx, dt = _nmm_plain(h, pre_g, sh1, sc1, w_zx, w_dt, tm, 1024)
    xa = _ssm_conv(zx, conv_w, conv_b, tm, 512)
    y_f, s_f = _ssd(xa, dt, *ssd_f, e2f, h0f, reverse=False)
    y, s_b = _ssd(xa, dt, *ssd_b, e2b, h0b, reverse=True, gate_args=(y_f, zx, norm_g))
    h = _post(y, w_out, jnp.zeros((d,), F32), post_g, g1, h, tm, 2048)
    return h, s_f, s_b


def _conf_stream(h, mods, pre_g, post_g, pw1_w, pw1_b, dw_w, dw_b, ln_g, ln_b, pw2_w, pw2_b,
                 seg, tm):
    sh1, sc1, g1 = mods
    tb = max(seg, 128)
    v = _nmm_glu(h, pre_g, sh1, sc1, pw1_w, pw1_b, tm, 1024)
    u = _conf_conv(v, dw_w, dw_b, ln_g, ln_b, seg, 256, tb, 512)
    return _post(u, pw2_w, pw2_b, post_g, g1, h, tm, pw2_w.shape[0])


def kernel(x, c, ctx, c_ctx, mod_w, mod_b, pre_mix_g, post_mix_g, pre_mlp_g, post_mlp_g, mlp_w1, mlp_w2, ssm_in_w, ssm_conv_w, ssm_conv_b, ssm_a_log_f, ssm_dt_bias_f, ssm_d_f, ssm_a_log_b, ssm_dt_bias_b, ssm_d_b, ssm_norm_g, ssm_out_w, conf_pw1_w, conf_pw1_b, conf_dw_w, conf_dw_b, conf_ln_g, conf_ln_b, conf_pw2_w, conf_pw2_b):
    bsz, seq_len, d = x.shape
    ctx_len = ctx.shape[1]
    depth = mod_w.shape[0]
    rows = seq_len // GRID_W
    tm, tmc = 512, ctx_len
    mods = _adaln(c, c_ctx, mod_w, mod_b)

    head_of_chan = jnp.arange(D_INNER, dtype=jnp.int32) // SSM_HEAD_DIM
    lane_head = jnp.arange(2 * SSM_HEADS, dtype=jnp.int32)
    e_f = (lane_head[:, None] == head_of_chan[None, :]).astype(BF16)
    e_b = (lane_head[:, None] == head_of_chan[None, :] + SSM_HEADS).astype(BF16)
    e2f = jnp.concatenate([e_f, e_f], axis=0)
    e2b = jnp.concatenate([e_b, e_b], axis=0)

    def is_col_major(i):
        return ((i // 2) % 2) == 1

    h, hc = x, ctx
    for i in range(depth):
        last = i == depth - 1
        kind = i % 2
        j = i // 2
        col_major = is_col_major(i)
        lat = [mods[i, :bsz, k][:, None, :] for k in range(6)]
        cm = [jnp.broadcast_to(mods[i, bsz, k][None, None, :], (bsz, 1, d)) for k in range(6)]
        if kind == 0:
            w_in = ssm_in_w[j].astype(BF16)
            w_zx, w_dt = w_in[:, :ZX], w_in[:, ZX:]
            a_log = jnp.concatenate([ssm_a_log_f[j], ssm_a_log_b[j]]).reshape(1, -1)
            dt_bias = jnp.concatenate([ssm_dt_bias_f[j], ssm_dt_bias_b[j]]).reshape(1, -1)
            ssd_f = (a_log, dt_bias, jnp.repeat(ssm_d_f[j], SSM_HEAD_DIM).reshape(1, -1))
            ssd_b = (a_log, dt_bias, jnp.repeat(ssm_d_b[j], SSM_HEAD_DIM).reshape(1, -1))
            common = (w_zx, w_dt, ssm_conv_w[j], ssm_conv_b[j], ssd_f, ssd_b, e2f, e2b,
                      ssm_norm_g[j], ssm_out_w[j].astype(BF16))
            zeros = jnp.zeros((bsz, SSM_STATE, D_INNER), F32)
            hc_new, s_f, s_b = _ssm_stream(hc, cm[:3], pre_mix_g[i], post_mix_g[i], *common,
                                           zeros, zeros, tmc)
            h, _, _ = _ssm_stream(h, lat[:3], pre_mix_g[i], post_mix_g[i], *common,
                                  s_f, s_b, tm)
        else:
            common = (conf_pw1_w[j].astype(BF16), conf_pw1_b[j], conf_dw_w[j], conf_dw_b[j],
                      conf_ln_g[j], conf_ln_b[j], conf_pw2_w[j].astype(BF16), conf_pw2_b[j])
            seg = rows if col_major else GRID_W
            if not last:
                hc_new = _conf_stream(hc, cm[:3], pre_mix_g[i], post_mix_g[i], *common,
                                      ctx_len, tmc)
            h = _conf_stream(h, lat[:3], pre_mix_g[i], post_mix_g[i], *common, seg, tm)
        w1 = mlp_w1[i].astype(BF16)
        w2 = mlp_w2[i].astype(BF16)
        next_col_major = False if last else is_col_major(i + 1)
        h = _mlp(h, pre_mlp_g[i], lat[3], lat[4], w1, w2, post_mlp_g[i], lat[5], tm, 1024,
                 in_t=col_major and not next_col_major, out_t=next_col_major and not col_major)
        if not last:
            hc = _mlp(hc_new, pre_mlp_g[i], cm[3], cm[4], w1, w2, post_mlp_g[i], cm[5], tmc, 1024)
    return h
```

```python
import functools

import jax
import jax.numpy as jnp
from jax import lax
from jax.experimental import pallas as pl
from jax.experimental.pallas import tpu as pltpu

F32 = jnp.float32
BF16 = jnp.bfloat16

EPS = 1e-6
GRID_W = 64
SSM_HEADS = 64
SSM_HEAD_DIM = 64
SSM_GROUPS = 8
SSM_STATE = 128
SSD_CHUNK = 128
D_INNER = SSM_HEADS * SSM_HEAD_DIM
GN = SSM_GROUPS * SSM_STATE
XBC = D_INNER + 2 * GN
ZX = D_INNER + XBC

V7X_VMEM_LIMIT = 56 * 1024 * 1024
F32_ROWS = 8
BF16_ROWS = 16
MAX_ROW_GROUPS = 16


def _cp(*sem):
    return pltpu.CompilerParams(dimension_semantics=sem, vmem_limit_bytes=V7X_VMEM_LIMIT)


def _dot(a, b):
    return jnp.dot(a, b, preferred_element_type=F32)


def _sigmoid(v):
    return 1.0 / (1.0 + jnp.exp(-v))


def _softplus(v):
    return jnp.maximum(v, 0.0) + jnp.log(1.0 + jnp.exp(-jnp.abs(v)))


def _row_access(ref, transposed, seg):
    if transposed:
        def load(r, n):
            return ref[pl.ds(r, n), seg, :]

        def store(r, v):
            ref[pl.ds(r, v.shape[0]), seg, :] = v
    else:
        def load(r, n):
            return ref[pl.ds(seg * GRID_W + r, n), :]

        def store(r, v):
            ref[pl.ds(seg * GRID_W + r, v.shape[0]), :] = v
    return load, store


def _norm_mod_rows(load, u_ref, u_row0, ntok, mul, sh):
    groups = min(MAX_ROW_GROUPS, ntok // F32_ROWS)
    step = F32_ROWS * groups

    def body(i, carry):
        r0 = pl.multiple_of(i * step, step)
        rs = []
        for s in range(groups):
            xs = load(r0 + F32_ROWS * s, F32_ROWS)
            rs.append(lax.rsqrt(jnp.mean(xs * xs, axis=-1, keepdims=True) + EPS))
        for s in range(0, groups, 2):
            ua = load(r0 + F32_ROWS * s, F32_ROWS) * rs[s] * mul + sh
            ub = load(r0 + F32_ROWS * (s + 1), F32_ROWS) * rs[s + 1] * mul + sh
            u_ref[pl.ds(u_row0 + r0 + F32_ROWS * s, BF16_ROWS), :] = (
                jnp.concatenate([ua, ub], axis=0).astype(BF16))
        return carry

    lax.fori_loop(0, ntok // step, body, 0)


def _post_rows(acc_ref, acc_row0, ntok, bias, mul, load_h, store_o):
    groups = min(MAX_ROW_GROUPS, ntok // F32_ROWS)
    step = F32_ROWS * groups

    def body(i, carry):
        r0 = pl.multiple_of(i * step, step)
        rs = []
        for s in range(groups):
            f = acc_ref[pl.ds(acc_row0 + r0 + F32_ROWS * s, F32_ROWS), :] + bias
            rs.append(lax.rsqrt(jnp.mean(f * f, axis=-1, keepdims=True) + EPS))
        for s in range(groups):
            f = acc_ref[pl.ds(acc_row0 + r0 + F32_ROWS * s, F32_ROWS), :] + bias
            store_o(r0 + F32_ROWS * s, load_h(r0 + F32_ROWS * s, F32_ROWS) + f * rs[s] * mul)
        return carry

    lax.fori_loop(0, ntok // step, body, 0)


def _adaln_kernel(c_ref, w_ref, b_ref, o_ref):
    cv = c_ref[...]
    s = cv * _sigmoid(cv)
    o_ref[0] = _dot(s.astype(BF16), w_ref[0].astype(BF16)) + b_ref[0]


def _adaln(c, c_ctx, mod_w, mod_b):
    depth, d, n = mod_w.shape
    bsz = c.shape[0]
    cc = jnp.zeros((8, d), F32).at[:bsz].set(c).at[bsz].set(c_ctx)
    tn = 1024
    out = pl.pallas_call(
        _adaln_kernel,
        out_shape=jax.ShapeDtypeStruct((depth, 8, n), F32),
        grid=(depth, n // tn),
        in_specs=[pl.BlockSpec((8, d), lambda i, j: (0, 0)),
                  pl.BlockSpec((1, d, tn), lambda i, j: (i, 0, j)),
                  pl.BlockSpec((1, 1, tn), lambda i, j: (i, 0, j))],
        out_specs=pl.BlockSpec((1, 8, tn), lambda i, j: (i, 0, j)),
        compiler_params=_cp("parallel", "parallel"),
        name="adaln",
    )(cc, mod_w, mod_b.reshape(depth, 1, n))
    return out.reshape(depth, 8, 6, d)


def _nmm_plain_kernel(x_ref, g_ref, sh_ref, sc_ref, w_ref, ws_ref, o_ref, os_ref, u_ref, *, tm):
    @pl.when(pl.program_id(2) == 0)
    def _():
        load, _ = _row_access(x_ref, False, 0)
        _norm_mod_rows(load, u_ref, 0, tm, g_ref[...] * (1.0 + sc_ref[...]), sh_ref[...])
        os_ref[...] = _dot(u_ref[...], ws_ref[...])

    o_ref[...] = _dot(u_ref[...], w_ref[...]).astype(o_ref.dtype)


def _nmm_plain(x, g, sh, sc, w, n, tm, tn):
    bsz, l, d = x.shape
    ns = w.shape[1] - n
    assert n % tn == 0 and n % ns == 0
    vec = pl.BlockSpec((1, d), lambda b, i, j: (0, 0))
    mod = pl.BlockSpec((None, 1, d), lambda b, i, j: (b, 0, 0))
    return pl.pallas_call(
        functools.partial(_nmm_plain_kernel, tm=tm),
        out_shape=(jax.ShapeDtypeStruct((bsz, l, n), BF16),
                   jax.ShapeDtypeStruct((bsz, l, ns), F32)),
        grid=(bsz, l // tm, n // tn),
        in_specs=[pl.BlockSpec((None, tm, d), lambda b, i, j: (b, i, 0)), vec, mod, mod,
                  pl.BlockSpec((d, tn), lambda b, i, j: (0, j)),
                  pl.BlockSpec((d, ns), lambda b, i, j: (0, n // ns))],
        out_specs=(pl.BlockSpec((None, tm, tn), lambda b, i, j: (b, i, j)),
                   pl.BlockSpec((None, tm, ns), lambda b, i, j: (b, i, 0))),
        scratch_shapes=[pltpu.VMEM((tm, d), BF16)],
        compiler_params=_cp("parallel", "parallel", "arbitrary"),
        name="norm_mod_matmul",
    )(x, g.reshape(1, d), sh, sc, w, w)


def _nmm_glu_kernel(x_ref, g_ref, sh_ref, sc_ref, wa_ref, wg_ref, ba_ref, bg_ref, o_ref, u_ref,
                    *, tm):
    @pl.when(pl.program_id(2) == 0)
    def _():
        load, _ = _row_access(x_ref, False, 0)
        _norm_mod_rows(load, u_ref, 0, tm, g_ref[...] * (1.0 + sc_ref[...]), sh_ref[...])

    a = _dot(u_ref[...], wa_ref[...]) + ba_ref[...]
    gt = _dot(u_ref[...], wg_ref[...]) + bg_ref[...]
    o_ref[...] = (a * _sigmoid(gt)).astype(o_ref.dtype)


def _nmm_glu(x, g, sh, sc, w, bias, tm, tn):
    bsz, l, d = x.shape
    n = w.shape[1] // 2
    nj = n // tn
    vec = pl.BlockSpec((1, d), lambda b, i, j: (0, 0))
    mod = pl.BlockSpec((None, 1, d), lambda b, i, j: (b, 0, 0))
    b2 = bias.reshape(1, 2 * n)
    return pl.pallas_call(
        functools.partial(_nmm_glu_kernel, tm=tm),
        out_shape=jax.ShapeDtypeStruct((bsz, l, n), BF16),
        grid=(bsz, l // tm, nj),
        in_specs=[pl.BlockSpec((None, tm, d), lambda b, i, j: (b, i, 0)), vec, mod, mod,
                  pl.BlockSpec((d, tn), lambda b, i, j: (0, j)),
                  pl.BlockSpec((d, tn), lambda b, i, j: (0, j + nj)),
                  pl.BlockSpec((1, tn), lambda b, i, j: (0, j)),
                  pl.BlockSpec((1, tn), lambda b, i, j: (0, j + nj))],
        out_specs=pl.BlockSpec((None, tm, tn), lambda b, i, j: (b, i, j)),
        scratch_shapes=[pltpu.VMEM((tm, d), BF16)],
        compiler_params=_cp("parallel", "parallel", "arbitrary"),
        name="norm_mod_glu",
    )(x, g.reshape(1, d), sh, sc, w, w, b2, b2)


def _post_kernel(y_ref, w_ref, b_ref, pg_ref, gate_ref, h_ref, o_ref, acc_ref, *, tm):
    k = pl.program_id(2)
    part = _dot(y_ref[...], w_ref[...])

    @pl.when(k == 0)
    def _():
        acc_ref[...] = part

    @pl.when(k > 0)
    def _():
        acc_ref[...] += part

    @pl.when(k == pl.num_programs(2) - 1)
    def _():
        load_h, _ = _row_access(h_ref, False, 0)
        _, store_o = _row_access(o_ref, False, 0)
        _post_rows(acc_ref, 0, tm, b_ref[...], pg_ref[...] * gate_ref[...], load_h, store_o)


def _post(y, w, bias, pg, gate, h, tm, tk):
    bsz, l, d = h.shape
    kdim = y.shape[2]
    vec = pl.BlockSpec((1, d), lambda b, i, k: (0, 0))
    mod = pl.BlockSpec((None, 1, d), lambda b, i, k: (b, 0, 0))
    hspec = pl.BlockSpec((None, tm, d), lambda b, i, k: (b, i, 0))
    return pl.pallas_call(
        functools.partial(_post_kernel, tm=tm),
        out_shape=jax.ShapeDtypeStruct(h.shape, F32),
        grid=(bsz, l // tm, kdim // tk),
        in_specs=[pl.BlockSpec((None, tm, tk), lambda b, i, k: (b, i, k)),
                  pl.BlockSpec((tk, d), lambda b, i, k: (k, 0)),
                  vec, vec, mod, hspec],
        out_specs=hspec,
        scratch_shapes=[pltpu.VMEM((tm, d), F32)],
        compiler_params=_cp("parallel", "parallel", "arbitrary"),
        name="matmul_post_norm_residual",
    )(y, w, bias.reshape(1, d), pg.reshape(1, d), gate, h)


def _mlp_kernel(x_ref, g_ref, sh_ref, sc_ref, w1_ref, w2_ref, pg_ref, gate_ref,
                o_ref, u_ref, acc_ref, *, tm, in_t, out_t):
    k = pl.program_id(2)
    if in_t or out_t:
        segs = [(s, GRID_W) for s in range(tm // GRID_W)]
    else:
        segs = [(0, tm)]

    @pl.when(k == 0)
    def _():
        mul = g_ref[...] * (1.0 + sc_ref[...])
        sh = sh_ref[...]
        for s, n in segs:
            load, _ = _row_access(x_ref, in_t, s)
            _norm_mod_rows(load, u_ref, s * GRID_W, n, mul, sh)

    hid = jnp.square(jnp.maximum(_dot(u_ref[...], w1_ref[...]), 0.0)).astype(BF16)
    part = _dot(hid, w2_ref[...])

    @pl.when(k == 0)
    def _():
        acc_ref[...] = part

    @pl.when(k > 0)
    def _():
        acc_ref[...] += part

    @pl.when(k == pl.num_programs(2) - 1)
    def _():
        mul = pg_ref[...] * gate_ref[...]
        for s, n in segs:
            load_h, _ = _row_access(x_ref, in_t, s)
            _, store_o = _row_access(o_ref, out_t, s)
            _post_rows(acc_ref, s * GRID_W, n, 0.0, mul, load_h, store_o)


def _mlp(h, g, sh, sc, w1, w2, pg, gate, tm, tf, in_t=False, out_t=False):
    bsz, l, d = h.shape
    ff = w1.shape[1]
    rows = l // GRID_W
    vec = pl.BlockSpec((1, d), lambda b, i, k: (0, 0))
    mod = pl.BlockSpec((None, 1, d), lambda b, i, k: (b, 0, 0))
    plain = pl.BlockSpec((None, tm, d), lambda b, i, k: (b, i, 0))
    if in_t or out_t:
        assert tm % GRID_W == 0 and (tm // GRID_W) % F32_ROWS == 0 and not (in_t and out_t)
    grid_view = pl.BlockSpec((None, GRID_W, tm // GRID_W, d), lambda b, i, k: (b, 0, i, 0))
    xin = h.reshape(bsz, GRID_W, rows, d) if in_t else h
    oshape = (bsz, GRID_W, rows, d) if out_t else (bsz, l, d)
    out = pl.pallas_call(
        functools.partial(_mlp_kernel, tm=tm, in_t=in_t, out_t=out_t),
        out_shape=jax.ShapeDtypeStruct(oshape, F32),
        grid=(bsz, l // tm, ff // tf),
        in_specs=[grid_view if in_t else plain, vec, mod, mod,
                  pl.BlockSpec((d, tf), lambda b, i, k: (0, k)),
                  pl.BlockSpec((tf, d), lambda b, i, k: (k, 0)),
                  vec, mod],
        out_specs=grid_view if out_t else plain,
        scratch_shapes=[pltpu.VMEM((tm, d), BF16), pltpu.VMEM((tm, d), F32)],
        compiler_params=_cp("parallel", "parallel", "arbitrary"),
        name="mlp_sq_relu",
    )(xin, g.reshape(1, d), sh, sc, w1, w2, pg.reshape(1, d), gate)
    return out.reshape(bsz, l, d)


def _shift_matrix(tb, kw, ext, pad, seg):
    t = jnp.arange(tb, dtype=jnp.int32)
    k = jnp.arange(kw, dtype=jnp.int32)
    e = jnp.arange(ext, dtype=jnp.int32) - pad
    src = t[:, None, None] + k[None, :, None] - kw // 2
    hit = src == e[None, None, :]
    if seg is not None:
        hit = hit & (src // seg == t[:, None, None] // seg)
    return hit.reshape(tb, kw * ext).astype(BF16)


def _scale_rows_by_taps(x, w_ref, xw_ref, row0, ext, kw):
    n, ch = x.shape
    x3 = x.reshape(n // F32_ROWS, F32_ROWS, ch)
    for k in range(kw):
        xw_ref[k * ext + row0:k * ext + row0 + n, :] = (
            (x3 * w_ref[k][None]).reshape(n, ch).astype(BF16))


def _ssm_conv_kernel(cur_ref, prev_ref, next_ref, s_ref, w_ref, b_ref, o_ref, xw_ref,
                     *, tl, tb, kw):
    i = pl.program_id(1)
    halo = BF16_ROWS
    ext = tb + 2 * halo
    rc = 32
    nblk = tl // tb
    keep_prev = jnp.where(i > 0, 1.0, 0.0)
    keep_next = jnp.where(i < pl.num_programs(1) - 1, 1.0, 0.0)
    for blk in range(nblk):
        t0 = blk * tb
        if blk == 0:
            before = prev_ref[...].astype(F32) * keep_prev
        else:
            before = cur_ref[t0 - halo:t0, :].astype(F32)
        if blk == nblk - 1:
            after = next_ref[...].astype(F32) * keep_next
        else:
            after = cur_ref[t0 + tb:t0 + tb + halo, :].astype(F32)
        _scale_rows_by_taps(before, w_ref, xw_ref, 0, ext, kw)
        for r in range(0, tb, rc):
            _scale_rows_by_taps(cur_ref[t0 + r:t0 + r + rc, :].astype(F32), w_ref, xw_ref,
                                halo + r, ext, kw)
        _scale_rows_by_taps(after, w_ref, xw_ref, halo + tb, ext, kw)
        acc = _dot(s_ref[...], xw_ref[...]) + b_ref[...]
        o_ref[t0:t0 + tb, :] = (acc * _sigmoid(acc)).astype(o_ref.dtype)


def _ssm_conv(zx, conv_w, conv_b, tl, tc):
    bsz, l, _ = zx.shape
    kw, ch = conv_w.shape
    tb = 128
    ext = tb + 2 * BF16_ROWS
    off = D_INNER // tc
    hb = tl // BF16_ROWS
    nhb = l // BF16_ROWS
    return pl.pallas_call(
        functools.partial(_ssm_conv_kernel, tl=tl, tb=tb, kw=kw),
        out_shape=jax.ShapeDtypeStruct((bsz, l, ch), BF16),
        grid=(bsz, l // tl, ch // tc),
        in_specs=[pl.BlockSpec((None, tl, tc), lambda b, i, j: (b, i, j + off)),
                  pl.BlockSpec((None, BF16_ROWS, tc),
                               lambda b, i, j: (b, jnp.maximum(i * hb - 1, 0), j + off)),
                  pl.BlockSpec((None, BF16_ROWS, tc),
                               lambda b, i, j: (b, jnp.minimum((i + 1) * hb, nhb - 1), j + off)),
                  pl.BlockSpec((tb, kw * ext), lambda b, i, j: (0, 0)),
                  pl.BlockSpec((kw, F32_ROWS, tc), lambda b, i, j: (0, 0, j)),
                  pl.BlockSpec((1, tc), lambda b, i, j: (0, j))],
        out_specs=pl.BlockSpec((None, tl, tc), lambda b, i, j: (b, i, j)),
        scratch_shapes=[pltpu.VMEM((kw * ext, tc), BF16)],
        compiler_params=_cp("parallel", "parallel", "parallel"),
        name="ssm_conv_silu",
    )(zx, zx, zx, _shift_matrix(tb, kw, ext, BF16_ROWS, None),
      jnp.broadcast_to(conv_w[:, None, :], (kw, F32_ROWS, ch)), conv_b.reshape(1, ch))


def _split2(v):
    hi = v.astype(BF16)
    return hi, (v - hi.astype(F32)).astype(BF16)


def _ssd_kernel(*refs, reverse, gated):
    if gated:
        (xbc_ref, dt_ref, alog_ref, dtb_ref, dsk_ref, e2_ref, h0_ref, yf_ref, z_ref, ng_ref,
         y_ref, hf_ref, s_ref, ybuf_ref) = refs
    else:
        (xbc_ref, dt_ref, alog_ref, dtb_ref, dsk_ref, e2_ref, h0_ref,
         y_ref, hf_ref, s_ref) = refs
        ybuf_ref = y_ref
    c = pl.program_id(1)
    q = SSD_CHUNK
    hp = SSM_HEAD_DIM
    gw = (SSM_HEADS // SSM_GROUPS) * hp
    lane0 = SSM_HEADS if reverse else 0
    end = 0 if reverse else q - 1

    @pl.when(c == 0)
    def _():
        s_ref[...] = h0_ref[...]

    dt = _softplus(dt_ref[...] + dtb_ref[...])
    dta = dt * (-jnp.exp(alog_ref[...]))
    ri = lax.broadcasted_iota(jnp.int32, (q, q), 0)
    ci = lax.broadcasted_iota(jnp.int32, (q, q), 1)
    tri = (ci >= ri) if reverse else (ci <= ri)
    tmat = jnp.where(tri, 1.0, 0.0).astype(BF16)
    p0 = dta.astype(BF16)
    r0 = dta - p0.astype(F32)
    p1 = r0.astype(BF16)
    p2 = (r0 - p1.astype(F32)).astype(BF16)
    cs = _dot(tmat, p0) + _dot(tmat, p1) + _dot(tmat, p2)
    cs_t = cs.T
    cs_end = cs[end:end + 1, :]
    e2 = e2_ref[...]

    def expand(v):
        hi, lo = _split2(v)
        return _dot(jnp.concatenate([hi, lo], axis=1), e2)

    dt_x = expand(dt)
    ecs_x = expand(jnp.exp(cs))
    dte_x = expand(jnp.exp(cs_end - cs))
    lane = lax.broadcasted_iota(jnp.int32, (q, 2 * hp), 1)

    for g in range(SSM_GROUPS):
        lo, hi = g * gw, (g + 1) * gw
        bg = xbc_ref[:, D_INNER + g * SSM_STATE:D_INNER + (g + 1) * SSM_STATE]
        cg = xbc_ref[:, D_INNER + GN + g * SSM_STATE:D_INNER + GN + (g + 1) * SSM_STATE]
        scores = lax.dot_general(cg, bg, (((1,), (1,)), ((), ())), preferred_element_type=F32)
        xs_g = xbc_ref[:, lo:hi].astype(F32)
        xdt_g = xs_g * dt_x[:, lo:hi]
        xdt_b = xdt_g.astype(BF16)
        s_g = s_ref[:, lo:hi]
        y_g = _dot(cg, s_g.astype(BF16)) * ecs_x[:, lo:hi] + dsk_ref[:, lo:hi] * xs_g
        s_ref[:, lo:hi] = ecs_x[end:end + 1, lo:hi] * s_g + lax.dot_general(
            bg, (xdt_g * dte_x[:, lo:hi]).astype(BF16), (((0,), (0,)), ((), ())),
            preferred_element_type=F32)
        for kp in range(gw // (2 * hp)):
            gmats = []
            for hh in range(2):
                hd = lane0 + g * (gw // hp) + 2 * kp + hh
                seg = cs[:, hd:hd + 1] - cs_t[hd:hd + 1, :]
                decay = jnp.exp(jnp.where(tri, seg, -jnp.inf))
                gmats.append((scores * decay).astype(BF16))
            xp = xdt_b[:, kp * 2 * hp:(kp + 1) * 2 * hp]
            rhs = jnp.concatenate([jnp.where(lane < hp, xp, jnp.zeros_like(xp)),
                                   jnp.where(lane >= hp, xp, jnp.zeros_like(xp))], axis=0)
            yd = _dot(jnp.concatenate(gmats, axis=1), rhs)
            cols = slice(lo + kp * 2 * hp, lo + (kp + 1) * 2 * hp)
            y_pair = yd + y_g[:, kp * 2 * hp:(kp + 1) * 2 * hp]
            if gated:
                zz = z_ref[:, cols].astype(F32)
                y_pair = (y_pair + yf_ref[:, cols]) * (zz * _sigmoid(zz))
            ybuf_ref[:, cols] = y_pair

    @pl.when(c == pl.num_programs(1) - 1)
    def _():
        hf_ref[...] = s_ref[...]

    if gated:
        load, _ = _row_access(ybuf_ref, False, 0)
        _norm_mod_rows(load, y_ref, 0, q, ng_ref[...], 0.0)


def _ssd(xa, dt, a_log, dt_bias, d_skip, e2, h0, reverse, gate_args=None):
    bsz, l, _ = xa.shape
    nc = l // SSD_CHUNK
    gated = gate_args is not None
    cmap = (lambda b, c: (b, nc - 1 - c, 0)) if reverse else (lambda b, c: (b, c, 0))
    const = lambda b, c: (0, 0)
    state = pl.BlockSpec((None, SSM_STATE, D_INNER), lambda b, c: (b, 0, 0))
    tok = pl.BlockSpec((None, SSD_CHUNK, D_INNER), cmap)
    in_specs = [pl.BlockSpec((None, SSD_CHUNK, XBC), cmap),
                pl.BlockSpec((None, SSD_CHUNK, 2 * SSM_HEADS), cmap),
                pl.BlockSpec((1, 2 * SSM_HEADS), const),
                pl.BlockSpec((1, 2 * SSM_HEADS), const),
                pl.BlockSpec((1, D_INNER), const),
                pl.BlockSpec((4 * SSM_HEADS, D_INNER), const),
                state]
    args = [xa, dt, a_log, dt_bias, d_skip, e2, h0]
    scratch = [pltpu.VMEM((SSM_STATE, D_INNER), F32)]
    if gated:
        y_fwd, zx, norm_g = gate_args
        in_specs += [tok, tok, pl.BlockSpec((1, D_INNER), const)]
        args += [y_fwd, zx, norm_g.reshape(1, D_INNER)]
        scratch.append(pltpu.VMEM((SSD_CHUNK, D_INNER), F32))
    return pl.pallas_call(
        functools.partial(_ssd_kernel, reverse=reverse, gated=gated),
        out_shape=(jax.ShapeDtypeStruct((bsz, l, D_INNER), BF16 if gated else F32),
                   jax.ShapeDtypeStruct((bsz, SSM_STATE, D_INNER), F32)),
        grid=(bsz, nc),
        in_specs=in_specs,
        out_specs=(tok, state),
        scratch_shapes=scratch,
        compiler_params=_cp("parallel", "arbitrary"),
        name="ssd_bwd_gate_norm" if gated else "ssd_fwd",
    )(*args)


def _conf_conv_kernel(v_ref, s_ref, w_ref, b_ref, lg_ref, lb_ref, o_ref, xw_ref, cv_ref,
                      *, tl, tb, kw, lc):
    j = pl.program_id(2)
    rc = 32
    for blk in range(tl // tb):
        for r in range(0, tb, rc):
            _scale_rows_by_taps(v_ref[blk * tb + r:blk * tb + r + rc, :].astype(F32),
                                w_ref, xw_ref, r, tb, kw)
        cv_ref[j, blk * tb:(blk + 1) * tb, :] = _dot(s_ref[...], xw_ref[...]) + b_ref[...]

    @pl.when(j == pl.num_programs(2) - 1)
    def _():
        nj = cv_ref.shape[0]
        inv_d = 1.0 / (nj * lc)
        groups = min(MAX_ROW_GROUPS, tl // F32_ROWS)
        step = F32_ROWS * groups

        def body(i, carry):
            r0 = pl.multiple_of(i * step, step)
            stats = []
            for s in range(groups):
                rows = pl.ds(r0 + F32_ROWS * s, F32_ROWS)
                tot = cv_ref[0, rows, :]
                for c in range(1, nj):
                    tot = tot + cv_ref[c, rows, :]
                mu = jnp.sum(tot, axis=-1, keepdims=True) * inv_d
                sq = (cv_ref[0, rows, :] - mu) * (cv_ref[0, rows, :] - mu)
                for c in range(1, nj):
                    sq = sq + (cv_ref[c, rows, :] - mu) * (cv_ref[c, rows, :] - mu)
                stats.append((mu, lax.rsqrt(jnp.sum(sq, axis=-1, keepdims=True) * inv_d + EPS)))
            for s in range(0, groups, 2):
                for c in range(nj):
                    halves = []
                    for t in (s, s + 1):
                        mu, rstd = stats[t]
                        xc = cv_ref[c, pl.ds(r0 + F32_ROWS * t, F32_ROWS), :] - mu
                        halves.append(xc * rstd * lg_ref[:, c * lc:(c + 1) * lc]
                                      + lb_ref[:, c * lc:(c + 1) * lc])
                    y = jnp.concatenate(halves, axis=0)
                    o_ref[pl.ds(r0 + F32_ROWS * s, BF16_ROWS), c * lc:(c + 1) * lc] = (
                        y * _sigmoid(y)).astype(BF16)
            return carry

        lax.fori_loop(0, tl // step, body, 0)


def _conf_conv(v, dw_w, dw_b, ln_g, ln_b, seg, tl, tb, lc):
    bsz, l, d = v.shape
    kw = dw_w.shape[0]
    vec = pl.BlockSpec((1, d), lambda b, i, j: (0, 0))
    return pl.pallas_call(
        functools.partial(_conf_conv_kernel, tl=tl, tb=tb, kw=kw, lc=lc),
        out_shape=jax.ShapeDtypeStruct((bsz, l, d), BF16),
        grid=(bsz, l // tl, d // lc),
        in_specs=[pl.BlockSpec((None, tl, lc), lambda b, i, j: (b, i, j)),
                  pl.BlockSpec((tb, kw * tb), lambda b, i, j: (0, 0)),
                  pl.BlockSpec((kw, F32_ROWS, lc), lambda b, i, j: (0, 0, j)),
                  pl.BlockSpec((1, lc), lambda b, i, j: (0, j)),
                  vec, vec],
        out_specs=pl.BlockSpec((None, tl, d), lambda b, i, j: (b, i, 0)),
        scratch_shapes=[pltpu.VMEM((kw * tb, lc), BF16), pltpu.VMEM((d // lc, tl, lc), F32)],
        compiler_params=_cp("parallel", "parallel", "arbitrary"),
        name="conf_conv_ln_swish",
    )(v, _shift_matrix(tb, kw, tb, 0, seg),
      jnp.broadcast_to(dw_w[:, None, :], (kw, F32_ROWS, d)),
      dw_b.reshape(1, d), ln_g.reshape(1, d), ln_b.reshape(1, d))


def _ssm_stream(h, mods, pre_g, post_g, w_in, conv_w, conv_b, ssd_f, ssd_b, e2f, e2b,
                norm_g, w_out, h0f, h0b, tm):
    sh1, sc1, g1 = mods
    d = h.shape[2]
    zx, dt = _nmm_plain(h, pre_g, sh1, sc1, w_in, ZX, tm, 2048)
    xa = _ssm_conv(zx, conv_w, conv_b, tm, 512)
    y_f, s_f = _ssd(xa, dt, *ssd_f, e2f, h0f, reverse=False)
    y, s_b = _ssd(xa, dt, *ssd_b, e2b, h0b, reverse=True, gate_args=(y_f, zx, norm_g))
    h = _post(y, w_out, jnp.zeros((d,), F32), post_g, g1, h, tm, 2048)
    return h, s_f, s_b


def _conf_stream(h, mods, pre_g, post_g, pw1_w, pw1_b, dw_w, dw_b, ln_g, ln_b, pw2_w, pw2_b,
                 seg, tm):
    sh1, sc1, g1 = mods
    tb = max(seg, 128)
    v = _nmm_glu(h, pre_g, sh1, sc1, pw1_w, pw1_b, tm, 1024)
    u = _conf_conv(v, dw_w, dw_b, ln_g, ln_b, seg, 256, tb, 512)
    return _post(u, pw2_w, pw2_b, post_g, g1, h, tm, pw2_w.shape[0])


def kernel(x, c, ctx, c_ctx, mod_w, mod_b, pre_mix_g, post_mix_g, pre_mlp_g, post_mlp_g, mlp_w1, mlp_w2, ssm_in_w, ssm_conv_w, ssm_conv_b, ssm_a_log_f, ssm_dt_bias_f, ssm_d_f, ssm_a_log_b, ssm_dt_bias_b, ssm_d_b, ssm_norm_g, ssm_out_w, conf_pw1_w, conf_pw1_b, conf_dw_w, conf_dw_b, conf_ln_g, conf_ln_b, conf_pw2_w, conf_pw2_b):
    bsz, seq_len, d = x.shape
    ctx_len = ctx.shape[1]
    depth = mod_w.shape[0]
    rows = seq_len // GRID_W
    tm, tmc = 512, ctx_len
    mods = _adaln(c, c_ctx, mod_w, mod_b)

    head_of_chan = jnp.arange(D_INNER, dtype=jnp.int32) // SSM_HEAD_DIM
    lane_head = jnp.arange(2 * SSM_HEADS, dtype=jnp.int32)
    e_f = (lane_head[:, None] == head_of_chan[None, :]).astype(BF16)
    e_b = (lane_head[:, None] == head_of_chan[None, :] + SSM_HEADS).astype(BF16)
    e2f = jnp.concatenate([e_f, e_f], axis=0)
    e2b = jnp.concatenate([e_b, e_b], axis=0)

    def is_col_major(i):
        return ((i // 2) % 2) == 1

    h, hc = x, ctx
    for i in range(depth):
        last = i == depth - 1
        kind = i % 2
        j = i // 2
        col_major = is_col_major(i)
        lat = [mods[i, :bsz, k][:, None, :] for k in range(6)]
        cm = [jnp.broadcast_to(mods[i, bsz, k][None, None, :], (bsz, 1, d)) for k in range(6)]
        if kind == 0:
            a_log = jnp.concatenate([ssm_a_log_f[j], ssm_a_log_b[j]]).reshape(1, -1)
            dt_bias = jnp.concatenate([ssm_dt_bias_f[j], ssm_dt_bias_b[j]]).reshape(1, -1)
            ssd_f = (a_log, dt_bias, jnp.repeat(ssm_d_f[j], SSM_HEAD_DIM).reshape(1, -1))
            ssd_b = (a_log, dt_bias, jnp.repeat(ssm_d_b[j], SSM_HEAD_DIM).reshape(1, -1))
            common = (ssm_in_w[j].astype(BF16), ssm_conv_w[j], ssm_conv_b[j], ssd_f, ssd_b, e2f, e2b,
                      ssm_norm_g[j], ssm_out_w[j].astype(BF16))
            zeros = jnp.zeros((bsz, SSM_STATE, D_INNER), F32)
            hc_new, s_f, s_b = _ssm_stream(hc, cm[:3], pre_mix_g[i], post_mix_g[i], *common,
                                           zeros, zeros, tmc)
            h, _, _ = _ssm_stream(h, lat[:3], pre_mix_g[i], post_mix_g[i], *common,
                                  s_f, s_b, tm)
        else:
            common = (conf_pw1_w[j].astype(BF16), conf_pw1_b[j], conf_dw_w[j], conf_dw_b[j],
                      conf_ln_g[j], conf_ln_b[j], conf_pw2_w[j].astype(BF16), conf_pw2_b[j])
            seg = rows if col_major else GRID_W
            if not last:
                hc_new = _conf_stream(hc, cm[:3], pre_mix_g[i], post_mix_g[i], *common,
                                      ctx_len, tmc)
            h = _conf_stream(h, lat[:3], pre_mix_g[i], post_mix_g[i], *common, seg, tm)
        w1 = mlp_w1[i].astype(BF16)
        w2 = mlp_w2[i].astype(BF16)
        next_col_major = False if last else is_col_major(i + 1)
        h = _mlp(h, pre_mlp_g[i], lat[3], lat[4], w1, w2, post_mlp_g[i], lat[5], tm, 1024,
                 in_t=col_major and not next_col_major, out_t=next_col_major and not col_major)
        if not last:
            hc = _mlp(hc_new, pre_mlp_g[i], cm[3], cm[4], w1, w2, post_mlp_g[i], cm[5], tmc, 1024)
    return h
```

```python
import functools

import jax
import jax.numpy as jnp
from jax import lax
from jax.experimental import pallas as pl
from jax.experimental.pallas import tpu as pltpu

F32 = jnp.float32
BF16 = jnp.bfloat16

EPS = 1e-6
GRID_W = 64
SSM_HEADS = 64
SSM_HEAD_DIM = 64
SSM_GROUPS = 8
SSM_STATE = 128
SSD_CHUNK = 128
D_INNER = SSM_HEADS * SSM_HEAD_DIM
GN = SSM_GROUPS * SSM_STATE
XBC = D_INNER + 2 * GN
ZX = D_INNER + XBC

V7X_VMEM_LIMIT = 56 * 1024 * 1024
F32_ROWS = 8
BF16_ROWS = 16
MAX_ROW_GROUPS = 16


def _cp(*sem):
    return pltpu.CompilerParams(dimension_semantics=sem, vmem_limit_bytes=V7X_VMEM_LIMIT)


def _dot(a, b):
    return jnp.dot(a, b, preferred_element_type=F32)


def _sigmoid(v):
    return 1.0 / (1.0 + jnp.exp(-v))


def _softplus(v):
    return jnp.maximum(v, 0.0) + jnp.log(1.0 + jnp.exp(-jnp.abs(v)))


def _row_access(ref, transposed, seg):
    if transposed:
        def load(r, n):
            return ref[pl.ds(r, n), seg, :]

        def store(r, v):
            ref[pl.ds(r, v.shape[0]), seg, :] = v
    else:
        def load(r, n):
            return ref[pl.ds(seg * GRID_W + r, n), :]

        def store(r, v):
            ref[pl.ds(seg * GRID_W + r, v.shape[0]), :] = v
    return load, store


def _norm_mod_rows(load, u_ref, u_row0, ntok, mul, sh):
    groups = min(MAX_ROW_GROUPS, ntok // F32_ROWS)
    step = F32_ROWS * groups

    def body(i, carry):
        r0 = pl.multiple_of(i * step, step)
        rs = []
        for s in range(groups):
            xs = load(r0 + F32_ROWS * s, F32_ROWS)
            rs.append(lax.rsqrt(jnp.mean(xs * xs, axis=-1, keepdims=True) + EPS))
        for s in range(0, groups, 2):
            ua = load(r0 + F32_ROWS * s, F32_ROWS) * rs[s] * mul + sh
            ub = load(r0 + F32_ROWS * (s + 1), F32_ROWS) * rs[s + 1] * mul + sh
            u_ref[pl.ds(u_row0 + r0 + F32_ROWS * s, BF16_ROWS), :] = (
                jnp.concatenate([ua, ub], axis=0).astype(BF16))
        return carry

    lax.fori_loop(0, ntok // step, body, 0)


def _post_rows(acc_ref, acc_row0, ntok, bias, mul, load_h, store_o):
    groups = min(MAX_ROW_GROUPS, ntok // F32_ROWS)
    step = F32_ROWS * groups

    def body(i, carry):
        r0 = pl.multiple_of(i * step, step)
        rs = []
        for s in range(groups):
            f = acc_ref[pl.ds(acc_row0 + r0 + F32_ROWS * s, F32_ROWS), :] + bias
            rs.append(lax.rsqrt(jnp.mean(f * f, axis=-1, keepdims=True) + EPS))
        for s in range(groups):
            f = acc_ref[pl.ds(acc_row0 + r0 + F32_ROWS * s, F32_ROWS), :] + bias
            store_o(r0 + F32_ROWS * s, load_h(r0 + F32_ROWS * s, F32_ROWS) + f * rs[s] * mul)
        return carry

    lax.fori_loop(0, ntok // step, body, 0)


def _adaln_kernel(c_ref, w_ref, b_ref, o_ref):
    cv = c_ref[...]
    s = cv * _sigmoid(cv)
    o_ref[0] = _dot(s.astype(BF16), w_ref[0].astype(BF16)) + b_ref[0]


def _adaln(c, c_ctx, mod_w, mod_b):
    depth, d, n = mod_w.shape
    bsz = c.shape[0]
    cc = jnp.zeros((8, d), F32).at[:bsz].set(c).at[bsz].set(c_ctx)
    tn = 1024
    out = pl.pallas_call(
        _adaln_kernel,
        out_shape=jax.ShapeDtypeStruct((depth, 8, n), F32),
        grid=(depth, n // tn),
        in_specs=[pl.BlockSpec((8, d), lambda i, j: (0, 0)),
                  pl.BlockSpec((1, d, tn), lambda i, j: (i, 0, j)),
                  pl.BlockSpec((1, 1, tn), lambda i, j: (i, 0, j))],
        out_specs=pl.BlockSpec((1, 8, tn), lambda i, j: (i, 0, j)),
        compiler_params=_cp("parallel", "parallel"),
        name="adaln",
    )(cc, mod_w, mod_b.reshape(depth, 1, n))
    return out.reshape(depth, 8, 6, d)


def _nmm_plain_kernel(x_ref, g_ref, sh_ref, sc_ref, w_ref, ws_ref, o_ref, os_ref, u_ref, *, tm):
    @pl.when(pl.program_id(2) == 0)
    def _():
        load, _ = _row_access(x_ref, False, 0)
        _norm_mod_rows(load, u_ref, 0, tm, g_ref[...] * (1.0 + sc_ref[...]), sh_ref[...])
        os_ref[...] = _dot(u_ref[...], ws_ref[...])

    o_ref[...] = _dot(u_ref[...], w_ref[...]).astype(o_ref.dtype)


def _nmm_plain(x, g, sh, sc, w, li, n, tm, tn):
    bsz, l, d = x.shape
    ns = w.shape[2] - n
    assert n % tn == 0 and n % ns == 0
    vec = pl.BlockSpec((1, d), lambda b, i, j: (0, 0))
    mod = pl.BlockSpec((None, 1, d), lambda b, i, j: (b, 0, 0))
    return pl.pallas_call(
        functools.partial(_nmm_plain_kernel, tm=tm),
        out_shape=(jax.ShapeDtypeStruct((bsz, l, n), BF16),
                   jax.ShapeDtypeStruct((bsz, l, ns), F32)),
        grid=(bsz, l // tm, n // tn),
        in_specs=[pl.BlockSpec((None, tm, d), lambda b, i, j: (b, i, 0)), vec, mod, mod,
                  pl.BlockSpec((None, d, tn), lambda b, i, j: (li, 0, j)),
                  pl.BlockSpec((None, d, ns), lambda b, i, j: (li, 0, n // ns))],
        out_specs=(pl.BlockSpec((None, tm, tn), lambda b, i, j: (b, i, j)),
                   pl.BlockSpec((None, tm, ns), lambda b, i, j: (b, i, 0))),
        scratch_shapes=[pltpu.VMEM((tm, d), BF16)],
        compiler_params=_cp("parallel", "parallel", "arbitrary"),
        name="norm_mod_matmul",
    )(x, g.reshape(1, d), sh, sc, w, w)


def _nmm_glu_kernel(x_ref, g_ref, sh_ref, sc_ref, wa_ref, wg_ref, ba_ref, bg_ref, o_ref, u_ref,
                    *, tm):
    @pl.when(pl.program_id(2) == 0)
    def _():
        load, _ = _row_access(x_ref, False, 0)
        _norm_mod_rows(load, u_ref, 0, tm, g_ref[...] * (1.0 + sc_ref[...]), sh_ref[...])

    a = _dot(u_ref[...], wa_ref[...]) + ba_ref[...]
    gt = _dot(u_ref[...], wg_ref[...]) + bg_ref[...]
    o_ref[...] = (a * _sigmoid(gt)).astype(o_ref.dtype)


def _nmm_glu(x, g, sh, sc, w, li, bias, tm, tn):
    bsz, l, d = x.shape
    n = w.shape[2] // 2
    nj = n // tn
    vec = pl.BlockSpec((1, d), lambda b, i, j: (0, 0))
    mod = pl.BlockSpec((None, 1, d), lambda b, i, j: (b, 0, 0))
    b2 = bias.reshape(1, 2 * n)
    return pl.pallas_call(
        functools.partial(_nmm_glu_kernel, tm=tm),
        out_shape=jax.ShapeDtypeStruct((bsz, l, n), BF16),
        grid=(bsz, l // tm, nj),
        in_specs=[pl.BlockSpec((None, tm, d), lambda b, i, j: (b, i, 0)), vec, mod, mod,
                  pl.BlockSpec((None, d, tn), lambda b, i, j: (li, 0, j)),
                  pl.BlockSpec((None, d, tn), lambda b, i, j: (li, 0, j + nj)),
                  pl.BlockSpec((1, tn), lambda b, i, j: (0, j)),
                  pl.BlockSpec((1, tn), lambda b, i, j: (0, j + nj))],
        out_specs=pl.BlockSpec((None, tm, tn), lambda b, i, j: (b, i, j)),
        scratch_shapes=[pltpu.VMEM((tm, d), BF16)],
        compiler_params=_cp("parallel", "parallel", "arbitrary"),
        name="norm_mod_glu",
    )(x, g.reshape(1, d), sh, sc, w, w, b2, b2)


def _post_kernel(y_ref, w_ref, b_ref, pg_ref, gate_ref, h_ref, o_ref, acc_ref, *, tm):
    k = pl.program_id(2)
    part = _dot(y_ref[...], w_ref[...])

    @pl.when(k == 0)
    def _():
        acc_ref[...] = part

    @pl.when(k > 0)
    def _():
        acc_ref[...] += part

    @pl.when(k == pl.num_programs(2) - 1)
    def _():
        load_h, _ = _row_access(h_ref, False, 0)
        _, store_o = _row_access(o_ref, False, 0)
        _post_rows(acc_ref, 0, tm, b_ref[...], pg_ref[...] * gate_ref[...], load_h, store_o)


def _post(y, w, li, bias, pg, gate, h, tm, tk):
    bsz, l, d = h.shape
    kdim = y.shape[2]
    vec = pl.BlockSpec((1, d), lambda b, i, k: (0, 0))
    mod = pl.BlockSpec((None, 1, d), lambda b, i, k: (b, 0, 0))
    hspec = pl.BlockSpec((None, tm, d), lambda b, i, k: (b, i, 0))
    return pl.pallas_call(
        functools.partial(_post_kernel, tm=tm),
        out_shape=jax.ShapeDtypeStruct(h.shape, F32),
        grid=(bsz, l // tm, kdim // tk),
        in_specs=[pl.BlockSpec((None, tm, tk), lambda b, i, k: (b, i, k)),
                  pl.BlockSpec((None, tk, d), lambda b, i, k: (li, k, 0)),
                  vec, vec, mod, hspec],
        out_specs=hspec,
        scratch_shapes=[pltpu.VMEM((tm, d), F32)],
        compiler_params=_cp("parallel", "parallel", "arbitrary"),
        name="matmul_post_norm_residual",
    )(y, w, bias.reshape(1, d), pg.reshape(1, d), gate, h)


def _mlp_kernel(x_ref, g_ref, sh_ref, sc_ref, w1_ref, w2_ref, pg_ref, gate_ref,
                o_ref, u_ref, acc_ref, *maybe_xtok_ref, tm, in_t, out_t):
    k = pl.program_id(2)
    segs = [(s, GRID_W) for s in range(tm // GRID_W)] if out_t else [(0, tm)]
    if in_t:
        xtok_ref, = maybe_xtok_ref
    else:
        xtok_ref = x_ref

    @pl.when(k == 0)
    def _():
        if in_t:
            for s in range(tm // GRID_W):
                for r in range(0, GRID_W, F32_ROWS):
                    xtok_ref[s * GRID_W + r:s * GRID_W + r + F32_ROWS, :] = (
                        x_ref[r:r + F32_ROWS, s, :])
        mul = g_ref[...] * (1.0 + sc_ref[...])
        sh = sh_ref[...]
        for s, n in segs:
            load, _ = _row_access(xtok_ref, False, s)
            _norm_mod_rows(load, u_ref, s * GRID_W, n, mul, sh)

    hid = jnp.square(jnp.maximum(_dot(u_ref[...], w1_ref[...]), 0.0)).astype(BF16)
    part = _dot(hid, w2_ref[...])

    @pl.when(k == 0)
    def _():
        acc_ref[...] = part

    @pl.when(k > 0)
    def _():
        acc_ref[...] += part

    @pl.when(k == pl.num_programs(2) - 1)
    def _():
        mul = pg_ref[...] * gate_ref[...]
        for s, n in segs:
            load_h, _ = _row_access(xtok_ref, False, s)
            _, store_o = _row_access(o_ref, out_t, s)
            _post_rows(acc_ref, s * GRID_W, n, 0.0, mul, load_h, store_o)


def _mlp(h, g, sh, sc, w1, w2, li, pg, gate, tm, tf, in_t=False, out_t=False):
    bsz, l, d = h.shape
    ff = w1.shape[2]
    rows = l // GRID_W
    vec = pl.BlockSpec((1, d), lambda b, i, k: (0, 0))
    mod = pl.BlockSpec((None, 1, d), lambda b, i, k: (b, 0, 0))
    plain = pl.BlockSpec((None, tm, d), lambda b, i, k: (b, i, 0))
    if in_t or out_t:
        assert tm % GRID_W == 0 and (tm // GRID_W) % F32_ROWS == 0 and not (in_t and out_t)
    grid_view = pl.BlockSpec((None, GRID_W, tm // GRID_W, d), lambda b, i, k: (b, 0, i, 0))
    xin = h.reshape(bsz, GRID_W, rows, d) if in_t else h
    oshape = (bsz, GRID_W, rows, d) if out_t else (bsz, l, d)
    out = pl.pallas_call(
        functools.partial(_mlp_kernel, tm=tm, in_t=in_t, out_t=out_t),
        out_shape=jax.ShapeDtypeStruct(oshape, F32),
        grid=(bsz, l // tm, ff // tf),
        in_specs=[grid_view if in_t else plain, vec, mod, mod,
                  pl.BlockSpec((None, d, tf), lambda b, i, k: (li, 0, k)),
                  pl.BlockSpec((None, tf, d), lambda b, i, k: (li, k, 0)),
                  vec, mod],
        out_specs=grid_view if out_t else plain,
        scratch_shapes=[pltpu.VMEM((tm, d), BF16), pltpu.VMEM((tm, d), F32)]
        + ([pltpu.VMEM((tm, d), F32)] if in_t else []),
        compiler_params=_cp("parallel", "parallel", "arbitrary"),
        name="mlp_sq_relu",
    )(xin, g.reshape(1, d), sh, sc, w1, w2, pg.reshape(1, d), gate)
    return out.reshape(bsz, l, d)


def _shift_matrix(tb, kw, ext, pad, seg):
    t = jnp.arange(tb, dtype=jnp.int32)
    k = jnp.arange(kw, dtype=jnp.int32)
    e = jnp.arange(ext, dtype=jnp.int32) - pad
    src = t[:, None, None] + k[None, :, None] - kw // 2
    hit = src == e[None, None, :]
    if seg is not None:
        hit = hit & (src // seg == t[:, None, None] // seg)
    return hit.reshape(tb, kw * ext).astype(BF16)


def _scale_rows_by_taps(x, w_ref, xw_ref, row0, ext, kw):
    n, ch = x.shape
    x3 = x.reshape(n // F32_ROWS, F32_ROWS, ch)
    for k in range(kw):
        xw_ref[k * ext + row0:k * ext + row0 + n, :] = (
            (x3 * w_ref[k][None]).reshape(n, ch).astype(BF16))


def _ssm_conv_kernel(cur_ref, prev_ref, next_ref, s_ref, w_ref, b_ref, o_ref, xw_ref,
                     *, tl, tb, kw):
    i = pl.program_id(1)
    halo = BF16_ROWS
    ext = tb + 2 * halo
    rc = 32
    nblk = tl // tb
    keep_prev = jnp.where(i > 0, 1.0, 0.0)
    keep_next = jnp.where(i < pl.num_programs(1) - 1, 1.0, 0.0)
    for blk in range(nblk):
        t0 = blk * tb
        if blk == 0:
            before = prev_ref[...].astype(F32) * keep_prev
        else:
            before = cur_ref[t0 - halo:t0, :].astype(F32)
        if blk == nblk - 1:
            after = next_ref[...].astype(F32) * keep_next
        else:
            after = cur_ref[t0 + tb:t0 + tb + halo, :].astype(F32)
        _scale_rows_by_taps(before, w_ref, xw_ref, 0, ext, kw)
        for r in range(0, tb, rc):
            _scale_rows_by_taps(cur_ref[t0 + r:t0 + r + rc, :].astype(F32), w_ref, xw_ref,
                                halo + r, ext, kw)
        _scale_rows_by_taps(after, w_ref, xw_ref, halo + tb, ext, kw)
        acc = _dot(s_ref[...], xw_ref[...]) + b_ref[...]
        o_ref[t0:t0 + tb, :] = (acc * _sigmoid(acc)).astype(o_ref.dtype)


def _ssm_conv(zx, conv_w, conv_b, tl, tc):
    bsz, l, _ = zx.shape
    kw, ch = conv_w.shape
    tb = 128
    ext = tb + 2 * BF16_ROWS
    off = D_INNER // tc
    hb = tl // BF16_ROWS
    nhb = l // BF16_ROWS
    return pl.pallas_call(
        functools.partial(_ssm_conv_kernel, tl=tl, tb=tb, kw=kw),
        out_shape=jax.ShapeDtypeStruct((bsz, l, ch), BF16),
        grid=(bsz, l // tl, ch // tc),
        in_specs=[pl.BlockSpec((None, tl, tc), lambda b, i, j: (b, i, j + off)),
                  pl.BlockSpec((None, BF16_ROWS, tc),
                               lambda b, i, j: (b, jnp.maximum(i * hb - 1, 0), j + off)),
                  pl.BlockSpec((None, BF16_ROWS, tc),
                               lambda b, i, j: (b, jnp.minimum((i + 1) * hb, nhb - 1), j + off)),
                  pl.BlockSpec((tb, kw * ext), lambda b, i, j: (0, 0)),
                  pl.BlockSpec((kw, F32_ROWS, tc), lambda b, i, j: (0, 0, j)),
                  pl.BlockSpec((1, tc), lambda b, i, j: (0, j))],
        out_specs=pl.BlockSpec((None, tl, tc), lambda b, i, j: (b, i, j)),
        scratch_shapes=[pltpu.VMEM((kw * ext, tc), BF16)],
        compiler_params=_cp("parallel", "parallel", "parallel"),
        name="ssm_conv_silu",
    )(zx, zx, zx, _shift_matrix(tb, kw, ext, BF16_ROWS, None),
      jnp.broadcast_to(conv_w[:, None, :], (kw, F32_ROWS, ch)), conv_b.reshape(1, ch))


def _split2(v):
    hi = v.astype(BF16)
    return hi, (v - hi.astype(F32)).astype(BF16)


def _ssd_kernel(*refs, reverse, gated):
    if gated:
        (xbc_ref, dt_ref, alog_ref, dtb_ref, dsk_ref, e2_ref, h0_ref, yf_ref, z_ref, ng_ref,
         y_ref, hf_ref, s_ref, ybuf_ref) = refs
    else:
        (xbc_ref, dt_ref, alog_ref, dtb_ref, dsk_ref, e2_ref, h0_ref,
         y_ref, hf_ref, s_ref) = refs
        ybuf_ref = y_ref
    c = pl.program_id(1)
    q = SSD_CHUNK
    hp = SSM_HEAD_DIM
    gw = (SSM_HEADS // SSM_GROUPS) * hp
    lane0 = SSM_HEADS if reverse else 0
    end = 0 if reverse else q - 1

    @pl.when(c == 0)
    def _():
        s_ref[...] = h0_ref[...]

    dt = _softplus(dt_ref[...] + dtb_ref[...])
    dta = dt * (-jnp.exp(alog_ref[...]))
    ri = lax.broadcasted_iota(jnp.int32, (q, q), 0)
    ci = lax.broadcasted_iota(jnp.int32, (q, q), 1)
    tri = (ci >= ri) if reverse else (ci <= ri)
    tmat = jnp.where(tri, 1.0, 0.0).astype(BF16)
    p0 = dta.astype(BF16)
    r0 = dta - p0.astype(F32)
    p1 = r0.astype(BF16)
    p2 = (r0 - p1.astype(F32)).astype(BF16)
    cs = _dot(tmat, p0) + _dot(tmat, p1) + _dot(tmat, p2)
    cs_t = cs.T
    cs_end = cs[end:end + 1, :]
    e2 = e2_ref[...]

    def expand(v):
        hi, lo = _split2(v)
        return _dot(jnp.concatenate([hi, lo], axis=1), e2)

    dt_x = expand(dt)
    ecs_x = expand(jnp.exp(cs))
    dte_x = expand(jnp.exp(cs_end - cs))
    lane = lax.broadcasted_iota(jnp.int32, (q, 2 * hp), 1)

    for g in range(SSM_GROUPS):
        lo, hi = g * gw, (g + 1) * gw
        bg = xbc_ref[:, D_INNER + g * SSM_STATE:D_INNER + (g + 1) * SSM_STATE]
        cg = xbc_ref[:, D_INNER + GN + g * SSM_STATE:D_INNER + GN + (g + 1) * SSM_STATE]
        scores = lax.dot_general(cg, bg, (((1,), (1,)), ((), ())), preferred_element_type=F32)
        xs_g = xbc_ref[:, lo:hi].astype(F32)
        xdt_g = xs_g * dt_x[:, lo:hi]
        xdt_b = xdt_g.astype(BF16)
        s_g = s_ref[:, lo:hi]
        y_g = _dot(cg, s_g.astype(BF16)) * ecs_x[:, lo:hi] + dsk_ref[:, lo:hi] * xs_g
        s_ref[:, lo:hi] = ecs_x[end:end + 1, lo:hi] * s_g + lax.dot_general(
            bg, (xdt_g * dte_x[:, lo:hi]).astype(BF16), (((0,), (0,)), ((), ())),
            preferred_element_type=F32)
        for kp in range(gw // (2 * hp)):
            gmats = []
            for hh in range(2):
                hd = lane0 + g * (gw // hp) + 2 * kp + hh
                seg = cs[:, hd:hd + 1] - cs_t[hd:hd + 1, :]
                decay = jnp.exp(jnp.where(tri, seg, -jnp.inf))
                gmats.append((scores * decay).astype(BF16))
            xp = xdt_b[:, kp * 2 * hp:(kp + 1) * 2 * hp]
            rhs = jnp.concatenate([jnp.where(lane < hp, xp, jnp.zeros_like(xp)),
                                   jnp.where(lane >= hp, xp, jnp.zeros_like(xp))], axis=0)
            yd = _dot(jnp.concatenate(gmats, axis=1), rhs)
            cols = slice(lo + kp * 2 * hp, lo + (kp + 1) * 2 * hp)
            y_pair = yd + y_g[:, kp * 2 * hp:(kp + 1) * 2 * hp]
            if gated:
                zz = z_ref[:, cols].astype(F32)
                y_pair = (y_pair + yf_ref[:, cols]) * (zz * _sigmoid(zz))
            ybuf_ref[:, cols] = y_pair

    @pl.when(c == pl.num_programs(1) - 1)
    def _():
        hf_ref[...] = s_ref[...]

    if gated:
        load, _ = _row_access(ybuf_ref, False, 0)
        _norm_mod_rows(load, y_ref, 0, q, ng_ref[...], 0.0)


def _ssd(xa, dt, a_log, dt_bias, d_skip, e2, h0, reverse, gate_args=None):
    bsz, l, _ = xa.shape
    nc = l // SSD_CHUNK
    gated = gate_args is not None
    cmap = (lambda b, c: (b, nc - 1 - c, 0)) if reverse else (lambda b, c: (b, c, 0))
    const = lambda b, c: (0, 0)
    state = pl.BlockSpec((None, SSM_STATE, D_INNER), lambda b, c: (b, 0, 0))
    tok = pl.BlockSpec((None, SSD_CHUNK, D_INNER), cmap)
    in_specs = [pl.BlockSpec((None, SSD_CHUNK, XBC), cmap),
                pl.BlockSpec((None, SSD_CHUNK, 2 * SSM_HEADS), cmap),
                pl.BlockSpec((1, 2 * SSM_HEADS), const),
                pl.BlockSpec((1, 2 * SSM_HEADS), const),
                pl.BlockSpec((1, D_INNER), const),
                pl.BlockSpec((4 * SSM_HEADS, D_INNER), const),
                state]
    args = [xa, dt, a_log, dt_bias, d_skip, e2, h0]
    scratch = [pltpu.VMEM((SSM_STATE, D_INNER), F32)]
    if gated:
        y_fwd, zx, norm_g = gate_args
        in_specs += [tok, tok, pl.BlockSpec((1, D_INNER), const)]
        args += [y_fwd, zx, norm_g.reshape(1, D_INNER)]
        scratch.append(pltpu.VMEM((SSD_CHUNK, D_INNER), F32))
    return pl.pallas_call(
        functools.partial(_ssd_kernel, reverse=reverse, gated=gated),
        out_shape=(jax.ShapeDtypeStruct((bsz, l, D_INNER), BF16 if gated else F32),
                   jax.ShapeDtypeStruct((bsz, SSM_STATE, D_INNER), F32)),
        grid=(bsz, nc),
        in_specs=in_specs,
        out_specs=(tok, state),
        scratch_shapes=scratch,
        compiler_params=_cp("parallel", "arbitrary"),
        name="ssd_bwd_gate_norm" if gated else "ssd_fwd",
    )(*args)


def _conf_conv_kernel(v_ref, s_ref, w_ref, b_ref, lg_ref, lb_ref, o_ref, xw_ref, cv_ref,
                      *, tl, tb, kw, lc):
    j = pl.program_id(2)
    rc = 32
    for blk in range(tl // tb):
        for r in range(0, tb, rc):
            _scale_rows_by_taps(v_ref[blk * tb + r:blk * tb + r + rc, :].astype(F32),
                                w_ref, xw_ref, r, tb, kw)
        cv_ref[j, blk * tb:(blk + 1) * tb, :] = _dot(s_ref[...], xw_ref[...]) + b_ref[...]

    @pl.when(j == pl.num_programs(2) - 1)
    def _():
        nj = cv_ref.shape[0]
        inv_d = 1.0 / (nj * lc)
        groups = min(MAX_ROW_GROUPS, tl // F32_ROWS)
        step = F32_ROWS * groups

        def body(i, carry):
            r0 = pl.multiple_of(i * step, step)
            stats = []
            for s in range(groups):
                rows = pl.ds(r0 + F32_ROWS * s, F32_ROWS)
                tot = cv_ref[0, rows, :]
                for c in range(1, nj):
                    tot = tot + cv_ref[c, rows, :]
                mu = jnp.sum(tot, axis=-1, keepdims=True) * inv_d
                sq = (cv_ref[0, rows, :] - mu) * (cv_ref[0, rows, :] - mu)
                for c in range(1, nj):
                    sq = sq + (cv_ref[c, rows, :] - mu) * (cv_ref[c, rows, :] - mu)
                stats.append((mu, lax.rsqrt(jnp.sum(sq, axis=-1, keepdims=True) * inv_d + EPS)))
            for s in range(0, groups, 2):
                for c in range(nj):
                    halves = []
                    for t in (s, s + 1):
                        mu, rstd = stats[t]
                        xc = cv_ref[c, pl.ds(r0 + F32_ROWS * t, F32_ROWS), :] - mu
                        halves.append(xc * rstd * lg_ref[:, c * lc:(c + 1) * lc]
                                      + lb_ref[:, c * lc:(c + 1) * lc])
                    y = jnp.concatenate(halves, axis=0)
                    o_ref[pl.ds(r0 + F32_ROWS * s, BF16_ROWS), c * lc:(c + 1) * lc] = (
                        y * _sigmoid(y)).astype(BF16)
            return carry

        lax.fori_loop(0, tl // step, body, 0)


def _conf_conv(v, dw_w, dw_b, ln_g, ln_b, seg, tl, tb, lc):
    bsz, l, d = v.shape
    kw = dw_w.shape[0]
    vec = pl.BlockSpec((1, d), lambda b, i, j: (0, 0))
    return pl.pallas_call(
        functools.partial(_conf_conv_kernel, tl=tl, tb=tb, kw=kw, lc=lc),
        out_shape=jax.ShapeDtypeStruct((bsz, l, d), BF16),
        grid=(bsz, l // tl, d // lc),
        in_specs=[pl.BlockSpec((None, tl, lc), lambda b, i, j: (b, i, j)),
                  pl.BlockSpec((tb, kw * tb), lambda b, i, j: (0, 0)),
                  pl.BlockSpec((kw, F32_ROWS, lc), lambda b, i, j: (0, 0, j)),
                  pl.BlockSpec((1, lc), lambda b, i, j: (0, j)),
                  vec, vec],
        out_specs=pl.BlockSpec((None, tl, d), lambda b, i, j: (b, i, 0)),
        scratch_shapes=[pltpu.VMEM((kw * tb, lc), BF16), pltpu.VMEM((d // lc, tl, lc), F32)],
        compiler_params=_cp("parallel", "parallel", "arbitrary"),
        name="conf_conv_ln_swish",
    )(v, _shift_matrix(tb, kw, tb, 0, seg),
      jnp.broadcast_to(dw_w[:, None, :], (kw, F32_ROWS, d)),
      dw_b.reshape(1, d), ln_g.reshape(1, d), ln_b.reshape(1, d))


def _ssm_stream(h, mods, pre_g, post_g, li, w_in, conv_w, conv_b, ssd_f, ssd_b, e2f, e2b,
                norm_g, w_out, h0f, h0b, tm):
    sh1, sc1, g1 = mods
    d = h.shape[2]
    zx, dt = _nmm_plain(h, pre_g, sh1, sc1, w_in, li, ZX, tm, 2048)
    xa = _ssm_conv(zx, conv_w, conv_b, tm, 512)
    y_f, s_f = _ssd(xa, dt, *ssd_f, e2f, h0f, reverse=False)
    y, s_b = _ssd(xa, dt, *ssd_b, e2b, h0b, reverse=True, gate_args=(y_f, zx, norm_g))
    h = _post(y, w_out, li, jnp.zeros((d,), F32), post_g, g1, h, tm, 2048)
    return h, s_f, s_b


def _conf_stream(h, mods, pre_g, post_g, li, pw1_w, pw1_b, dw_w, dw_b, ln_g, ln_b, pw2_w, pw2_b,
                 seg, tm):
    sh1, sc1, g1 = mods
    tb = max(seg, 128)
    v = _nmm_glu(h, pre_g, sh1, sc1, pw1_w, li, pw1_b, tm, 1024)
    u = _conf_conv(v, dw_w, dw_b, ln_g, ln_b, seg, 256, tb, 512)
    return _post(u, pw2_w, li, pw2_b, post_g, g1, h, tm, pw2_w.shape[1])


def kernel(x, c, ctx, c_ctx, mod_w, mod_b, pre_mix_g, post_mix_g, pre_mlp_g, post_mlp_g, mlp_w1, mlp_w2, ssm_in_w, ssm_conv_w, ssm_conv_b, ssm_a_log_f, ssm_dt_bias_f, ssm_d_f, ssm_a_log_b, ssm_dt_bias_b, ssm_d_b, ssm_norm_g, ssm_out_w, conf_pw1_w, conf_pw1_b, conf_dw_w, conf_dw_b, conf_ln_g, conf_ln_b, conf_pw2_w, conf_pw2_b):
    bsz, seq_len, d = x.shape
    ctx_len = ctx.shape[1]
    depth = mod_w.shape[0]
    rows = seq_len // GRID_W
    tm, tmc = 512, ctx_len
    mods = _adaln(c, c_ctx, mod_w, mod_b)
    w1, w2 = mlp_w1.astype(BF16), mlp_w2.astype(BF16)
    ssm_in_b, ssm_out_b = ssm_in_w.astype(BF16), ssm_out_w.astype(BF16)
    conf_pw1_b16, conf_pw2_b16 = conf_pw1_w.astype(BF16), conf_pw2_w.astype(BF16)

    head_of_chan = jnp.arange(D_INNER, dtype=jnp.int32) // SSM_HEAD_DIM
    lane_head = jnp.arange(2 * SSM_HEADS, dtype=jnp.int32)
    e_f = (lane_head[:, None] == head_of_chan[None, :]).astype(BF16)
    e_b = (lane_head[:, None] == head_of_chan[None, :] + SSM_HEADS).astype(BF16)
    e2f = jnp.concatenate([e_f, e_f], axis=0)
    e2b = jnp.concatenate([e_b, e_b], axis=0)

    def is_col_major(i):
        return ((i // 2) % 2) == 1

    h, hc = x, ctx
    for i in range(depth):
        last = i == depth - 1
        kind = i % 2
        j = i // 2
        col_major = is_col_major(i)
        lat = [mods[i, :bsz, k][:, None, :] for k in range(6)]
        cm = [jnp.broadcast_to(mods[i, bsz, k][None, None, :], (bsz, 1, d)) for k in range(6)]
        if kind == 0:
            a_log = jnp.concatenate([ssm_a_log_f[j], ssm_a_log_b[j]]).reshape(1, -1)
            dt_bias = jnp.concatenate([ssm_dt_bias_f[j], ssm_dt_bias_b[j]]).reshape(1, -1)
            ssd_f = (a_log, dt_bias, jnp.repeat(ssm_d_f[j], SSM_HEAD_DIM).reshape(1, -1))
            ssd_b = (a_log, dt_bias, jnp.repeat(ssm_d_b[j], SSM_HEAD_DIM).reshape(1, -1))
            common = (j, ssm_in_b, ssm_conv_w[j], ssm_conv_b[j], ssd_f, ssd_b, e2f, e2b,
                      ssm_norm_g[j], ssm_out_b)
            zeros = jnp.zeros((bsz, SSM_STATE, D_INNER), F32)
            hc_new, s_f, s_b = _ssm_stream(hc, cm[:3], pre_mix_g[i], post_mix_g[i], *common,
                                           zeros, zeros, tmc)
            h, _, _ = _ssm_stream(h, lat[:3], pre_mix_g[i], post_mix_g[i], *common,
                                  s_f, s_b, tm)
        else:
            common = (j, conf_pw1_b16, conf_pw1_b[j], conf_dw_w[j], conf_dw_b[j],
                      conf_ln_g[j], conf_ln_b[j], conf_pw2_b16, conf_pw2_b[j])
            seg = rows if col_major else GRID_W
            if not last:
                hc_new = _conf_stream(hc, cm[:3], pre_mix_g[i], post_mix_g[i], *common,
                                      ctx_len, tmc)
            h = _conf_stream(h, lat[:3], pre_mix_g[i], post_mix_g[i], *common, seg, tm)
        next_col_major = False if last else is_col_major(i + 1)
        h = _mlp(h, pre_mlp_g[i], lat[3], lat[4], w1, w2, i, post_mlp_g[i], lat[5], tm, 1024,
                 in_t=col_major and not next_col_major, out_t=next_col_major and not col_major)
        if not last:
            hc = _mlp(hc_new, pre_mlp_g[i], cm[3], cm[4], w1, w2, i, post_mlp_g[i], cm[5], tmc, 1024)
    return h
```

```python
import functools

import jax
import jax.numpy as jnp
from jax import lax
from jax.experimental import pallas as pl
from jax.experimental.pallas import tpu as pltpu

F32 = jnp.float32
BF16 = jnp.bfloat16

EPS = 1e-6
GRID_W = 64
SSM_HEADS = 64
SSM_HEAD_DIM = 64
SSM_GROUPS = 8
SSM_STATE = 128
SSD_CHUNK = 128
D_INNER = SSM_HEADS * SSM_HEAD_DIM
GN = SSM_GROUPS * SSM_STATE
XBC = D_INNER + 2 * GN
ZX = D_INNER + XBC

V7X_VMEM_LIMIT = 56 * 1024 * 1024
F32_ROWS = 8
BF16_ROWS = 16
MAX_ROW_GROUPS = 16


def _cp(*sem):
    return pltpu.CompilerParams(dimension_semantics=sem, vmem_limit_bytes=V7X_VMEM_LIMIT)


def _dot(a, b):
    return jnp.dot(a, b, preferred_element_type=F32)


def _sigmoid(v):
    return 1.0 / (1.0 + jnp.exp(-v))


def _softplus(v):
    return jnp.maximum(v, 0.0) + jnp.log(1.0 + jnp.exp(-jnp.abs(v)))


def _row_access(ref, transposed, seg):
    if transposed:
        def load(r, n):
            return ref[pl.ds(r, n), seg, :]

        def store(r, v):
            ref[pl.ds(r, v.shape[0]), seg, :] = v
    else:
        def load(r, n):
            return ref[pl.ds(seg * GRID_W + r, n), :]

        def store(r, v):
            ref[pl.ds(seg * GRID_W + r, v.shape[0]), :] = v
    return load, store


def _norm_mod_rows(load, u_ref, u_row0, ntok, mul, sh):
    groups = min(MAX_ROW_GROUPS, ntok // F32_ROWS)
    step = F32_ROWS * groups

    def body(i, carry):
        r0 = pl.multiple_of(i * step, step)
        rs = []
        for s in range(groups):
            xs = load(r0 + F32_ROWS * s, F32_ROWS)
            rs.append(lax.rsqrt(jnp.mean(xs * xs, axis=-1, keepdims=True) + EPS))
        for s in range(0, groups, 2):
            ua = load(r0 + F32_ROWS * s, F32_ROWS) * rs[s] * mul + sh
            ub = load(r0 + F32_ROWS * (s + 1), F32_ROWS) * rs[s + 1] * mul + sh
            u_ref[pl.ds(u_row0 + r0 + F32_ROWS * s, BF16_ROWS), :] = (
                jnp.concatenate([ua, ub], axis=0).astype(BF16))
        return carry

    lax.fori_loop(0, ntok // step, body, 0)


def _post_rows(acc_ref, acc_row0, ntok, bias, mul, load_h, store_o):
    groups = min(MAX_ROW_GROUPS, ntok // F32_ROWS)
    step = F32_ROWS * groups

    def body(i, carry):
        r0 = pl.multiple_of(i * step, step)
        rs = []
        for s in range(groups):
            f = acc_ref[pl.ds(acc_row0 + r0 + F32_ROWS * s, F32_ROWS), :] + bias
            rs.append(lax.rsqrt(jnp.mean(f * f, axis=-1, keepdims=True) + EPS))
        for s in range(groups):
            f = acc_ref[pl.ds(acc_row0 + r0 + F32_ROWS * s, F32_ROWS), :] + bias
            store_o(r0 + F32_ROWS * s, load_h(r0 + F32_ROWS * s, F32_ROWS) + f * rs[s] * mul)
        return carry

    lax.fori_loop(0, ntok // step, body, 0)


def _adaln_kernel(c_ref, w_ref, b_ref, o_ref):
    cv = c_ref[...]
    s = cv * _sigmoid(cv)
    o_ref[0] = _dot(s.astype(BF16), w_ref[0].astype(BF16)) + b_ref[0]


def _adaln(c, c_ctx, mod_w, mod_b):
    depth, d, n = mod_w.shape
    bsz = c.shape[0]
    cc = jnp.zeros((8, d), F32).at[:bsz].set(c).at[bsz].set(c_ctx)
    tn = 1024
    out = pl.pallas_call(
        _adaln_kernel,
        out_shape=jax.ShapeDtypeStruct((depth, 8, n), F32),
        grid=(depth, n // tn),
        in_specs=[pl.BlockSpec((8, d), lambda i, j: (0, 0)),
                  pl.BlockSpec((1, d, tn), lambda i, j: (i, 0, j)),
                  pl.BlockSpec((1, 1, tn), lambda i, j: (i, 0, j))],
        out_specs=pl.BlockSpec((1, 8, tn), lambda i, j: (i, 0, j)),
        compiler_params=_cp("parallel", "parallel"),
        name="adaln",
    )(cc, mod_w, mod_b.reshape(depth, 1, n))
    return out.reshape(depth, 8, 6, d)


def _nmm_plain_kernel(x_ref, g_ref, sh_ref, sc_ref, w_ref, ws_ref, o_ref, os_ref, u_ref, *, tm):
    @pl.when(pl.program_id(2) == 0)
    def _():
        load, _ = _row_access(x_ref, False, 0)
        _norm_mod_rows(load, u_ref, 0, tm, g_ref[...] * (1.0 + sc_ref[...]), sh_ref[...])
        os_ref[...] = _dot(u_ref[...], ws_ref[...])

    o_ref[...] = _dot(u_ref[...], w_ref[...]).astype(o_ref.dtype)


def _nmm_plain(x, g, sh, sc, w, li, n, tm, tn):
    bsz, l, d = x.shape
    ns = w.shape[2] - n
    assert n % tn == 0 and n % ns == 0
    vec = pl.BlockSpec((1, d), lambda b, i, j: (0, 0))
    mod = pl.BlockSpec((None, 1, d), lambda b, i, j: (b, 0, 0))
    return pl.pallas_call(
        functools.partial(_nmm_plain_kernel, tm=tm),
        out_shape=(jax.ShapeDtypeStruct((bsz, l, n), BF16),
                   jax.ShapeDtypeStruct((bsz, l, ns), F32)),
        grid=(bsz, l // tm, n // tn),
        in_specs=[pl.BlockSpec((None, tm, d), lambda b, i, j: (b, i, 0)), vec, mod, mod,
                  pl.BlockSpec((None, d, tn), lambda b, i, j: (li, 0, j)),
                  pl.BlockSpec((None, d, ns), lambda b, i, j: (li, 0, n // ns))],
        out_specs=(pl.BlockSpec((None, tm, tn), lambda b, i, j: (b, i, j)),
                   pl.BlockSpec((None, tm, ns), lambda b, i, j: (b, i, 0))),
        scratch_shapes=[pltpu.VMEM((tm, d), BF16)],
        compiler_params=_cp("parallel", "parallel", "arbitrary"),
        name="norm_mod_matmul",
    )(x, g.reshape(1, d), sh, sc, w, w)


def _nmm_glu_kernel(x_ref, g_ref, sh_ref, sc_ref, wa_ref, wg_ref, ba_ref, bg_ref, o_ref, u_ref,
                    *, tm):
    @pl.when(pl.program_id(2) == 0)
    def _():
        load, _ = _row_access(x_ref, False, 0)
        _norm_mod_rows(load, u_ref, 0, tm, g_ref[...] * (1.0 + sc_ref[...]), sh_ref[...])

    a = _dot(u_ref[...], wa_ref[...]) + ba_ref[...]
    gt = _dot(u_ref[...], wg_ref[...]) + bg_ref[...]
    o_ref[...] = (a * _sigmoid(gt)).astype(o_ref.dtype)


def _nmm_glu(x, g, sh, sc, w, li, bias, tm, tn):
    bsz, l, d = x.shape
    n = w.shape[2] // 2
    nj = n // tn
    vec = pl.BlockSpec((1, d), lambda b, i, j: (0, 0))
    mod = pl.BlockSpec((None, 1, d), lambda b, i, j: (b, 0, 0))
    b2 = bias.reshape(1, 2 * n)
    return pl.pallas_call(
        functools.partial(_nmm_glu_kernel, tm=tm),
        out_shape=jax.ShapeDtypeStruct((bsz, l, n), BF16),
        grid=(bsz, l // tm, nj),
        in_specs=[pl.BlockSpec((None, tm, d), lambda b, i, j: (b, i, 0)), vec, mod, mod,
                  pl.BlockSpec((None, d, tn), lambda b, i, j: (li, 0, j)),
                  pl.BlockSpec((None, d, tn), lambda b, i, j: (li, 0, j + nj)),
                  pl.BlockSpec((1, tn), lambda b, i, j: (0, j)),
                  pl.BlockSpec((1, tn), lambda b, i, j: (0, j + nj))],
        out_specs=pl.BlockSpec((None, tm, tn), lambda b, i, j: (b, i, j)),
        scratch_shapes=[pltpu.VMEM((tm, d), BF16)],
        compiler_params=_cp("parallel", "parallel", "arbitrary"),
        name="norm_mod_glu",
    )(x, g.reshape(1, d), sh, sc, w, w, b2, b2)


def _post_kernel(y_ref, w_ref, b_ref, pg_ref, gate_ref, h_ref, o_ref, acc_ref, *, tm):
    k = pl.program_id(2)
    part = _dot(y_ref[...], w_ref[...])

    @pl.when(k == 0)
    def _():
        acc_ref[...] = part

    @pl.when(k > 0)
    def _():
        acc_ref[...] += part

    @pl.when(k == pl.num_programs(2) - 1)
    def _():
        load_h, _ = _row_access(h_ref, False, 0)
        _, store_o = _row_access(o_ref, False, 0)
        _post_rows(acc_ref, 0, tm, b_ref[...], pg_ref[...] * gate_ref[...], load_h, store_o)


def _post(y, w, li, bias, pg, gate, h, tm, tk):
    bsz, l, d = h.shape
    kdim = y.shape[2]
    vec = pl.BlockSpec((1, d), lambda b, i, k: (0, 0))
    mod = pl.BlockSpec((None, 1, d), lambda b, i, k: (b, 0, 0))
    hspec = pl.BlockSpec((None, tm, d), lambda b, i, k: (b, i, 0))
    return pl.pallas_call(
        functools.partial(_post_kernel, tm=tm),
        out_shape=jax.ShapeDtypeStruct(h.shape, F32),
        grid=(bsz, l // tm, kdim // tk),
        in_specs=[pl.BlockSpec((None, tm, tk), lambda b, i, k: (b, i, k)),
                  pl.BlockSpec((None, tk, d), lambda b, i, k: (li, k, 0)),
                  vec, vec, mod, hspec],
        out_specs=hspec,
        scratch_shapes=[pltpu.VMEM((tm, d), F32)],
        compiler_params=_cp("parallel", "parallel", "arbitrary"),
        name="matmul_post_norm_residual",
    )(y, w, bias.reshape(1, d), pg.reshape(1, d), gate, h)


def _mlp_kernel(x_ref, g_ref, sh_ref, sc_ref, w1_ref, w2_ref, pg_ref, gate_ref,
                o_ref, u_ref, acc_ref, *maybe_xtok_ref, tm, in_t, out_t):
    k = pl.program_id(2)
    segs = [(s, GRID_W) for s in range(tm // GRID_W)] if out_t else [(0, tm)]
    if in_t:
        xtok_ref, = maybe_xtok_ref
    else:
        xtok_ref = x_ref

    @pl.when(k == 0)
    def _():
        if in_t:
            for s in range(tm // GRID_W):
                for r in range(0, GRID_W, F32_ROWS):
                    xtok_ref[s * GRID_W + r:s * GRID_W + r + F32_ROWS, :] = (
                        x_ref[r:r + F32_ROWS, s, :])
        mul = g_ref[...] * (1.0 + sc_ref[...])
        sh = sh_ref[...]
        for s, n in segs:
            load, _ = _row_access(xtok_ref, False, s)
            _norm_mod_rows(load, u_ref, s * GRID_W, n, mul, sh)

    hid = jnp.square(jnp.maximum(_dot(u_ref[...], w1_ref[...]), 0.0)).astype(BF16)
    part = _dot(hid, w2_ref[...])

    @pl.when(k == 0)
    def _():
        acc_ref[...] = part

    @pl.when(k > 0)
    def _():
        acc_ref[...] += part

    @pl.when(k == pl.num_programs(2) - 1)
    def _():
        mul = pg_ref[...] * gate_ref[...]
        for s, n in segs:
            load_h, _ = _row_access(xtok_ref, False, s)
            _, store_o = _row_access(o_ref, out_t, s)
            _post_rows(acc_ref, s * GRID_W, n, 0.0, mul, load_h, store_o)


def _mlp(h, g, sh, sc, w1, w2, li, pg, gate, tm, tf, in_t=False, out_t=False):
    bsz, l, d = h.shape
    ff = w1.shape[2]
    rows = l // GRID_W
    vec = pl.BlockSpec((1, d), lambda b, i, k: (0, 0))
    mod = pl.BlockSpec((None, 1, d), lambda b, i, k: (b, 0, 0))
    plain = pl.BlockSpec((None, tm, d), lambda b, i, k: (b, i, 0))
    if in_t or out_t:
        assert tm % GRID_W == 0 and (tm // GRID_W) % F32_ROWS == 0 and not (in_t and out_t)
    grid_view = pl.BlockSpec((None, GRID_W, tm // GRID_W, d), lambda b, i, k: (b, 0, i, 0))
    xin = h.reshape(bsz, GRID_W, rows, d) if in_t else h
    oshape = (bsz, GRID_W, rows, d) if out_t else (bsz, l, d)
    out = pl.pallas_call(
        functools.partial(_mlp_kernel, tm=tm, in_t=in_t, out_t=out_t),
        out_shape=jax.ShapeDtypeStruct(oshape, F32),
        grid=(bsz, l // tm, ff // tf),
        in_specs=[grid_view if in_t else plain, vec, mod, mod,
                  pl.BlockSpec((None, d, tf), lambda b, i, k: (li, 0, k)),
                  pl.BlockSpec((None, tf, d), lambda b, i, k: (li, k, 0)),
                  vec, mod],
        out_specs=grid_view if out_t else plain,
        scratch_shapes=[pltpu.VMEM((tm, d), BF16), pltpu.VMEM((tm, d), F32)]
        + ([pltpu.VMEM((tm, d), F32)] if in_t else []),
        compiler_params=_cp("parallel", "parallel", "arbitrary"),
        name="mlp_sq_relu",
    )(xin, g.reshape(1, d), sh, sc, w1, w2, pg.reshape(1, d), gate)
    return out.reshape(bsz, l, d)


def _shift_matrix(tb, kw, ext, pad, seg):
    t = jnp.arange(tb, dtype=jnp.int32)
    k = jnp.arange(kw, dtype=jnp.int32)
    e = jnp.arange(ext, dtype=jnp.int32) - pad
    src = t[:, None, None] + k[None, :, None] - kw // 2
    hit = src == e[None, None, :]
    if seg is not None:
        hit = hit & (src // seg == t[:, None, None] // seg)
    return hit.reshape(tb, kw * ext).astype(BF16)


def _scale_rows_by_taps(x, w_ref, xw_ref, row0, ext, kw):
    n, ch = x.shape
    x3 = x.reshape(n // BF16_ROWS, BF16_ROWS, ch)
    for k in range(kw):
        xw_ref[k * ext + row0:k * ext + row0 + n, :] = (x3 * w_ref[k][None]).reshape(n, ch)


def _ssm_conv_kernel(cur_ref, prev_ref, next_ref, s_ref, w_ref, b_ref, o_ref, xw_ref,
                     *, tl, tb, kw):
    i = pl.program_id(1)
    halo = BF16_ROWS
    ext = tb + 2 * halo
    rc = 32
    nblk = tl // tb
    keep_prev = jnp.where(i > 0, 1.0, 0.0).astype(BF16)
    keep_next = jnp.where(i < pl.num_programs(1) - 1, 1.0, 0.0).astype(BF16)
    for blk in range(nblk):
        t0 = blk * tb
        if blk == 0:
            before = prev_ref[...] * keep_prev
        else:
            before = cur_ref[t0 - halo:t0, :]
        if blk == nblk - 1:
            after = next_ref[...] * keep_next
        else:
            after = cur_ref[t0 + tb:t0 + tb + halo, :]
        _scale_rows_by_taps(before, w_ref, xw_ref, 0, ext, kw)
        for r in range(0, tb, rc):
            _scale_rows_by_taps(cur_ref[t0 + r:t0 + r + rc, :], w_ref, xw_ref,
                                halo + r, ext, kw)
        _scale_rows_by_taps(after, w_ref, xw_ref, halo + tb, ext, kw)
        acc = _dot(s_ref[...], xw_ref[...]) + b_ref[...]
        o_ref[t0:t0 + tb, :] = (acc * _sigmoid(acc)).astype(o_ref.dtype)


def _ssm_conv(zx, conv_w, conv_b, tl, tc):
    bsz, l, _ = zx.shape
    kw, ch = conv_w.shape
    tb = 128
    ext = tb + 2 * BF16_ROWS
    off = D_INNER // tc
    hb = tl // BF16_ROWS
    nhb = l // BF16_ROWS
    return pl.pallas_call(
        functools.partial(_ssm_conv_kernel, tl=tl, tb=tb, kw=kw),
        out_shape=jax.ShapeDtypeStruct((bsz, l, ch), BF16),
        grid=(bsz, l // tl, ch // tc),
        in_specs=[pl.BlockSpec((None, tl, tc), lambda b, i, j: (b, i, j + off)),
                  pl.BlockSpec((None, BF16_ROWS, tc),
                               lambda b, i, j: (b, jnp.maximum(i * hb - 1, 0), j + off)),
                  pl.BlockSpec((None, BF16_ROWS, tc),
                               lambda b, i, j: (b, jnp.minimum((i + 1) * hb, nhb - 1), j + off)),
                  pl.BlockSpec((tb, kw * ext), lambda b, i, j: (0, 0)),
                  pl.BlockSpec((kw, BF16_ROWS, tc), lambda b, i, j: (0, 0, j)),
                  pl.BlockSpec((1, tc), lambda b, i, j: (0, j))],
        out_specs=pl.BlockSpec((None, tl, tc), lambda b, i, j: (b, i, j)),
        scratch_shapes=[pltpu.VMEM((kw * ext, tc), BF16)],
        compiler_params=_cp("parallel", "parallel", "parallel"),
        name="ssm_conv_silu",
    )(zx, zx, zx, _shift_matrix(tb, kw, ext, BF16_ROWS, None),
      jnp.broadcast_to(conv_w.astype(BF16)[:, None, :], (kw, BF16_ROWS, ch)),
      conv_b.reshape(1, ch))


def _split2(v):
    hi = v.astype(BF16)
    return hi, (v - hi.astype(F32)).astype(BF16)


def _ssd_kernel(*refs, reverse, gated):
    if gated:
        (xbc_ref, dt_ref, alog_ref, dtb_ref, dsk_ref, e2_ref, h0_ref, yf_ref, z_ref, ng_ref,
         y_ref, hf_ref, s_ref, ybuf_ref) = refs
    else:
        (xbc_ref, dt_ref, alog_ref, dtb_ref, dsk_ref, e2_ref, h0_ref,
         y_ref, hf_ref, s_ref) = refs
        ybuf_ref = y_ref
    c = pl.program_id(1)
    q = SSD_CHUNK
    hp = SSM_HEAD_DIM
    gw = (SSM_HEADS // SSM_GROUPS) * hp
    lane0 = SSM_HEADS if reverse else 0
    end = 0 if reverse else q - 1

    @pl.when(c == 0)
    def _():
        s_ref[...] = h0_ref[...]

    dt = _softplus(dt_ref[...] + dtb_ref[...])
    dta = dt * (-jnp.exp(alog_ref[...]))
    ri = lax.broadcasted_iota(jnp.int32, (q, q), 0)
    ci = lax.broadcasted_iota(jnp.int32, (q, q), 1)
    tri = (ci >= ri) if reverse else (ci <= ri)
    tmat = jnp.where(tri, 1.0, 0.0).astype(BF16)
    p0 = dta.astype(BF16)
    r0 = dta - p0.astype(F32)
    p1 = r0.astype(BF16)
    p2 = (r0 - p1.astype(F32)).astype(BF16)
    cs = _dot(tmat, p0) + _dot(tmat, p1) + _dot(tmat, p2)
    cs_t = cs.T
    cs_end = cs[end:end + 1, :]
    e2 = e2_ref[...]

    def expand(v):
        hi, lo = _split2(v)
        return _dot(jnp.concatenate([hi, lo], axis=1), e2)

    dt_x = expand(dt)
    ecs_x = expand(jnp.exp(cs))
    dte_x = expand(jnp.exp(cs_end - cs))
    lane = lax.broadcasted_iota(jnp.int32, (q, 2 * hp), 1)

    for g in range(SSM_GROUPS):
        lo, hi = g * gw, (g + 1) * gw
        bg = xbc_ref[:, D_INNER + g * SSM_STATE:D_INNER + (g + 1) * SSM_STATE]
        cg = xbc_ref[:, D_INNER + GN + g * SSM_STATE:D_INNER + GN + (g + 1) * SSM_STATE]
        scores = lax.dot_general(cg, bg, (((1,), (1,)), ((), ())), preferred_element_type=F32)
        xs_g = xbc_ref[:, lo:hi].astype(F32)
        xdt_g = xs_g * dt_x[:, lo:hi]
        xdt_b = xdt_g.astype(BF16)
        s_g = s_ref[:, lo:hi]
        y_g = _dot(cg, s_g.astype(BF16)) * ecs_x[:, lo:hi] + dsk_ref[:, lo:hi] * xs_g
        s_ref[:, lo:hi] = ecs_x[end:end + 1, lo:hi] * s_g + lax.dot_general(
            bg, (xdt_g * dte_x[:, lo:hi]).astype(BF16), (((0,), (0,)), ((), ())),
            preferred_element_type=F32)
        for kp in range(gw // (2 * hp)):
            gmats = []
            for hh in range(2):
                hd = lane0 + g * (gw // hp) + 2 * kp + hh
                seg = cs[:, hd:hd + 1] - cs_t[hd:hd + 1, :]
                decay = jnp.exp(jnp.where(tri, seg, -jnp.inf))
                gmats.append((scores * decay).astype(BF16))
            xp = xdt_b[:, kp * 2 * hp:(kp + 1) * 2 * hp]
            rhs = jnp.concatenate([jnp.where(lane < hp, xp, jnp.zeros_like(xp)),
                                   jnp.where(lane >= hp, xp, jnp.zeros_like(xp))], axis=0)
            yd = _dot(jnp.concatenate(gmats, axis=1), rhs)
            cols = slice(lo + kp * 2 * hp, lo + (kp + 1) * 2 * hp)
            y_pair = yd + y_g[:, kp * 2 * hp:(kp + 1) * 2 * hp]
            if gated:
                zz = z_ref[:, cols].astype(F32)
                y_pair = (y_pair + yf_ref[:, cols]) * (zz * _sigmoid(zz))
            ybuf_ref[:, cols] = y_pair

    @pl.when(c == pl.num_programs(1) - 1)
    def _():
        hf_ref[...] = s_ref[...]

    if gated:
        load, _ = _row_access(ybuf_ref, False, 0)
        _norm_mod_rows(load, y_ref, 0, q, ng_ref[...], 0.0)


def _ssd(xa, dt, a_log, dt_bias, d_skip, e2, h0, reverse, gate_args=None):
    bsz, l, _ = xa.shape
    nc = l // SSD_CHUNK
    gated = gate_args is not None
    cmap = (lambda b, c: (b, nc - 1 - c, 0)) if reverse else (lambda b, c: (b, c, 0))
    const = lambda b, c: (0, 0)
    state = pl.BlockSpec((None, SSM_STATE, D_INNER), lambda b, c: (b, 0, 0))
    tok = pl.BlockSpec((None, SSD_CHUNK, D_INNER), cmap)
    in_specs = [pl.BlockSpec((None, SSD_CHUNK, XBC), cmap),
                pl.BlockSpec((None, SSD_CHUNK, 2 * SSM_HEADS), cmap),
                pl.BlockSpec((1, 2 * SSM_HEADS), const),
                pl.BlockSpec((1, 2 * SSM_HEADS), const),
                pl.BlockSpec((1, D_INNER), const),
                pl.BlockSpec((4 * SSM_HEADS, D_INNER), const),
                state]
    args = [xa, dt, a_log, dt_bias, d_skip, e2, h0]
    scratch = [pltpu.VMEM((SSM_STATE, D_INNER), F32)]
    if gated:
        y_fwd, zx, norm_g = gate_args
        in_specs += [tok, tok, pl.BlockSpec((1, D_INNER), const)]
        args += [y_fwd, zx, norm_g.reshape(1, D_INNER)]
        scratch.append(pltpu.VMEM((SSD_CHUNK, D_INNER), F32))
    return pl.pallas_call(
        functools.partial(_ssd_kernel, reverse=reverse, gated=gated),
        out_shape=(jax.ShapeDtypeStruct((bsz, l, D_INNER), BF16 if gated else F32),
                   jax.ShapeDtypeStruct((bsz, SSM_STATE, D_INNER), F32)),
        grid=(bsz, nc),
        in_specs=in_specs,
        out_specs=(tok, state),
        scratch_shapes=scratch,
        compiler_params=_cp("parallel", "arbitrary"),
        name="ssd_bwd_gate_norm" if gated else "ssd_fwd",
    )(*args)


def _conf_conv_kernel(v_ref, s_ref, w_ref, b_ref, lg_ref, lb_ref, o_ref, xw_ref, cv_ref,
                      *, tl, tb, kw, lc):
    j = pl.program_id(2)
    rc = 32
    for blk in range(tl // tb):
        for r in range(0, tb, rc):
            _scale_rows_by_taps(v_ref[blk * tb + r:blk * tb + r + rc, :],
                                w_ref, xw_ref, r, tb, kw)
        cv_ref[j, blk * tb:(blk + 1) * tb, :] = _dot(s_ref[...], xw_ref[...]) + b_ref[...]

    @pl.when(j == pl.num_programs(2) - 1)
    def _():
        nj = cv_ref.shape[0]
        inv_d = 1.0 / (nj * lc)
        groups = min(MAX_ROW_GROUPS, tl // F32_ROWS)
        step = F32_ROWS * groups

        def body(i, carry):
            r0 = pl.multiple_of(i * step, step)
            stats = []
            for s in range(groups):
                rows = pl.ds(r0 + F32_ROWS * s, F32_ROWS)
                tot = cv_ref[0, rows, :]
                for c in range(1, nj):
                    tot = tot + cv_ref[c, rows, :]
                mu = jnp.sum(tot, axis=-1, keepdims=True) * inv_d
                sq = (cv_ref[0, rows, :] - mu) * (cv_ref[0, rows, :] - mu)
                for c in range(1, nj):
                    sq = sq + (cv_ref[c, rows, :] - mu) * (cv_ref[c, rows, :] - mu)
                stats.append((mu, lax.rsqrt(jnp.sum(sq, axis=-1, keepdims=True) * inv_d + EPS)))
            for s in range(0, groups, 2):
                for c in range(nj):
                    halves = []
                    for t in (s, s + 1):
                        mu, rstd = stats[t]
                        xc = cv_ref[c, pl.ds(r0 + F32_ROWS * t, F32_ROWS), :] - mu
                        halves.append(xc * rstd * lg_ref[:, c * lc:(c + 1) * lc]
                                      + lb_ref[:, c * lc:(c + 1) * lc])
                    y = jnp.concatenate(halves, axis=0)
                    o_ref[pl.ds(r0 + F32_ROWS * s, BF16_ROWS), c * lc:(c + 1) * lc] = (
                        y * _sigmoid(y)).astype(BF16)
            return carry

        lax.fori_loop(0, tl // step, body, 0)


def _conf_conv(v, dw_w, dw_b, ln_g, ln_b, seg, tl, tb, lc):
    bsz, l, d = v.shape
    kw = dw_w.shape[0]
    vec = pl.BlockSpec((1, d), lambda b, i, j: (0, 0))
    return pl.pallas_call(
        functools.partial(_conf_conv_kernel, tl=tl, tb=tb, kw=kw, lc=lc),
        out_shape=jax.ShapeDtypeStruct((bsz, l, d), BF16),
        grid=(bsz, l // tl, d // lc),
        in_specs=[pl.BlockSpec((None, tl, lc), lambda b, i, j: (b, i, j)),
                  pl.BlockSpec((tb, kw * tb), lambda b, i, j: (0, 0)),
                  pl.BlockSpec((kw, BF16_ROWS, lc), lambda b, i, j: (0, 0, j)),
                  pl.BlockSpec((1, lc), lambda b, i, j: (0, j)),
                  vec, vec],
        out_specs=pl.BlockSpec((None, tl, d), lambda b, i, j: (b, i, 0)),
        scratch_shapes=[pltpu.VMEM((kw * tb, lc), BF16), pltpu.VMEM((d // lc, tl, lc), F32)],
        compiler_params=_cp("parallel", "parallel", "arbitrary"),
        name="conf_conv_ln_swish",
    )(v, _shift_matrix(tb, kw, tb, 0, seg),
      jnp.broadcast_to(dw_w.astype(BF16)[:, None, :], (kw, BF16_ROWS, d)),
      dw_b.reshape(1, d), ln_g.reshape(1, d), ln_b.reshape(1, d))


def _ssm_stream(h, mods, pre_g, post_g, li, w_in, conv_w, conv_b, ssd_f, ssd_b, e2f, e2b,
                norm_g, w_out, h0f, h0b, tm):
    sh1, sc1, g1 = mods
    d = h.shape[2]
    zx, dt = _nmm_plain(h, pre_g, sh1, sc1, w_in, li, ZX, tm, 2048)
    xa = _ssm_conv(zx, conv_w, conv_b, tm, 1024)
    y_f, s_f = _ssd(xa, dt, *ssd_f, e2f, h0f, reverse=False)
    y, s_b = _ssd(xa, dt, *ssd_b, e2b, h0b, reverse=True, gate_args=(y_f, zx, norm_g))
    h = _post(y, w_out, li, jnp.zeros((d,), F32), post_g, g1, h, tm, 2048)
    return h, s_f, s_b


def _conf_stream(h, mods, pre_g, post_g, li, pw1_w, pw1_b, dw_w, dw_b, ln_g, ln_b, pw2_w, pw2_b,
                 seg, tm):
    sh1, sc1, g1 = mods
    tb = max(seg, 128)
    v = _nmm_glu(h, pre_g, sh1, sc1, pw1_w, li, pw1_b, tm, 1024)
    u = _conf_conv(v, dw_w, dw_b, ln_g, ln_b, seg, 256, tb, 1024)
    return _post(u, pw2_w, li, pw2_b, post_g, g1, h, tm, pw2_w.shape[1])


def kernel(x, c, ctx, c_ctx, mod_w, mod_b, pre_mix_g, post_mix_g, pre_mlp_g, post_mlp_g, mlp_w1, mlp_w2, ssm_in_w, ssm_conv_w, ssm_conv_b, ssm_a_log_f, ssm_dt_bias_f, ssm_d_f, ssm_a_log_b, ssm_dt_bias_b, ssm_d_b, ssm_norm_g, ssm_out_w, conf_pw1_w, conf_pw1_b, conf_dw_w, conf_dw_b, conf_ln_g, conf_ln_b, conf_pw2_w, conf_pw2_b):
    bsz, seq_len, d = x.shape
    ctx_len = ctx.shape[1]
    depth = mod_w.shape[0]
    rows = seq_len // GRID_W
    tm, tmc = 512, ctx_len
    mods = _adaln(c, c_ctx, mod_w, mod_b)
    w1, w2 = mlp_w1.astype(BF16), mlp_w2.astype(BF16)
    ssm_in_b, ssm_out_b = ssm_in_w.astype(BF16), ssm_out_w.astype(BF16)
    conf_pw1_b16, conf_pw2_b16 = conf_pw1_w.astype(BF16), conf_pw2_w.astype(BF16)

    head_of_chan = jnp.arange(D_INNER, dtype=jnp.int32) // SSM_HEAD_DIM
    lane_head = jnp.arange(2 * SSM_HEADS, dtype=jnp.int32)
    e_f = (lane_head[:, None] == head_of_chan[None, :]).astype(BF16)
    e_b = (lane_head[:, None] == head_of_chan[None, :] + SSM_HEADS).astype(BF16)
    e2f = jnp.concatenate([e_f, e_f], axis=0)
    e2b = jnp.concatenate([e_b, e_b], axis=0)

    def is_col_major(i):
        return ((i // 2) % 2) == 1

    h, hc = x, ctx
    for i in range(depth):
        last = i == depth - 1
        kind = i % 2
        j = i // 2
        col_major = is_col_major(i)
        lat = [mods[i, :bsz, k][:, None, :] for k in range(6)]
        cm = [jnp.broadcast_to(mods[i, bsz, k][None, None, :], (bsz, 1, d)) for k in range(6)]
        if kind == 0:
            a_log = jnp.concatenate([ssm_a_log_f[j], ssm_a_log_b[j]]).reshape(1, -1)
            dt_bias = jnp.concatenate([ssm_dt_bias_f[j], ssm_dt_bias_b[j]]).reshape(1, -1)
            ssd_f = (a_log, dt_bias, jnp.repeat(ssm_d_f[j], SSM_HEAD_DIM).reshape(1, -1))
            ssd_b = (a_log, dt_bias, jnp.repeat(ssm_d_b[j], SSM_HEAD_DIM).reshape(1, -1))
            common = (j, ssm_in_b, ssm_conv_w[j], ssm_conv_b[j], ssd_f, ssd_b, e2f, e2b,
                      ssm_norm_g[j], ssm_out_b)
            zeros = jnp.zeros((bsz, SSM_STATE, D_INNER), F32)
            hc_new, s_f, s_b = _ssm_stream(hc, cm[:3], pre_mix_g[i], post_mix_g[i], *common,
                                           zeros, zeros, tmc)
            h, _, _ = _ssm_stream(h, lat[:3], pre_mix_g[i], post_mix_g[i], *common,
                                  s_f, s_b, tm)
        else:
            common = (j, conf_pw1_b16, conf_pw1_b[j], conf_dw_w[j], conf_dw_b[j],
                      conf_ln_g[j], conf_ln_b[j], conf_pw2_b16, conf_pw2_b[j])
            seg = rows if col_major else GRID_W
            if not last:
                hc_new = _conf_stream(hc, cm[:3], pre_mix_g[i], post_mix_g[i], *common,
                                      ctx_len, tmc)
            h = _conf_stream(h, lat[:3], pre_mix_g[i], post_mix_g[i], *common, seg, tm)
        next_col_major = False if last else is_col_major(i + 1)
        h = _mlp(h, pre_mlp_g[i], lat[3], lat[4], w1, w2, i, post_mlp_g[i], lat[5], tm, 1024,
                 in_t=col_major and not next_col_major, out_t=next_col_major and not col_major)
        if not last:
            hc = _mlp(hc_new, pre_mlp_g[i], cm[3], cm[4], w1, w2, i, post_mlp_g[i], cm[5], tmc, 1024)
    return h
```

```python
import functools

import jax
import jax.numpy as jnp
from jax import lax
from jax.experimental import pallas as pl
from jax.experimental.pallas import tpu as pltpu

F32 = jnp.float32
BF16 = jnp.bfloat16

EPS = 1e-6
GRID_W = 64
SSM_HEADS = 64
SSM_HEAD_DIM = 64
SSM_GROUPS = 8
SSM_STATE = 128
SSD_CHUNK = 128
D_INNER = SSM_HEADS * SSM_HEAD_DIM
GN = SSM_GROUPS * SSM_STATE
XBC = D_INNER + 2 * GN
ZX = D_INNER + XBC

V7X_VMEM_LIMIT = 56 * 1024 * 1024
F32_ROWS = 8
BF16_ROWS = 16
MAX_ROW_GROUPS = 16


def _cp(*sem):
    return pltpu.CompilerParams(dimension_semantics=sem, vmem_limit_bytes=V7X_VMEM_LIMIT)


def _dot(a, b):
    return jnp.dot(a, b, preferred_element_type=F32)


def _sigmoid(v):
    return 1.0 / (1.0 + jnp.exp(-v))


def _softplus(v):
    return jnp.maximum(v, 0.0) + jnp.log(1.0 + jnp.exp(-jnp.abs(v)))


def _row_access(ref, transposed, seg):
    if transposed:
        def load(r, n):
            return ref[pl.ds(r, n), seg, :]

        def store(r, v):
            ref[pl.ds(r, v.shape[0]), seg, :] = v
    else:
        def load(r, n):
            return ref[pl.ds(seg * GRID_W + r, n), :]

        def store(r, v):
            ref[pl.ds(seg * GRID_W + r, v.shape[0]), :] = v
    return load, store


def _norm_mod_rows(load, u_ref, u_row0, ntok, mul, sh):
    groups = min(MAX_ROW_GROUPS, ntok // F32_ROWS)
    step = F32_ROWS * groups

    def body(i, carry):
        r0 = pl.multiple_of(i * step, step)
        rs = []
        for s in range(groups):
            xs = load(r0 + F32_ROWS * s, F32_ROWS)
            rs.append(lax.rsqrt(jnp.mean(xs * xs, axis=-1, keepdims=True) + EPS))
        for s in range(0, groups, 2):
            ua = load(r0 + F32_ROWS * s, F32_ROWS) * rs[s] * mul + sh
            ub = load(r0 + F32_ROWS * (s + 1), F32_ROWS) * rs[s + 1] * mul + sh
            u_ref[pl.ds(u_row0 + r0 + F32_ROWS * s, BF16_ROWS), :] = (
                jnp.concatenate([ua, ub], axis=0).astype(BF16))
        return carry

    lax.fori_loop(0, ntok // step, body, 0)


def _post_rows(acc_ref, acc_row0, ntok, bias, mul, load_h, store_o):
    groups = min(MAX_ROW_GROUPS, ntok // F32_ROWS)
    step = F32_ROWS * groups

    def body(i, carry):
        r0 = pl.multiple_of(i * step, step)
        rs = []
        for s in range(groups):
            f = acc_ref[pl.ds(acc_row0 + r0 + F32_ROWS * s, F32_ROWS), :] + bias
            rs.append(lax.rsqrt(jnp.mean(f * f, axis=-1, keepdims=True) + EPS))
        for s in range(groups):
            f = acc_ref[pl.ds(acc_row0 + r0 + F32_ROWS * s, F32_ROWS), :] + bias
            store_o(r0 + F32_ROWS * s, load_h(r0 + F32_ROWS * s, F32_ROWS) + f * rs[s] * mul)
        return carry

    lax.fori_loop(0, ntok // step, body, 0)


def _adaln_kernel(c_ref, w_ref, b_ref, o_ref):
    cv = c_ref[...]
    s = cv * _sigmoid(cv)
    o_ref[0] = _dot(s.astype(BF16), w_ref[0].astype(BF16)) + b_ref[0]


def _adaln(c, c_ctx, mod_w, mod_b):
    depth, d, n = mod_w.shape
    bsz = c.shape[0]
    cc = jnp.zeros((8, d), F32).at[:bsz].set(c).at[bsz].set(c_ctx)
    tn = 1024
    out = pl.pallas_call(
        _adaln_kernel,
        out_shape=jax.ShapeDtypeStruct((depth, 8, n), F32),
        grid=(depth, n // tn),
        in_specs=[pl.BlockSpec((8, d), lambda i, j: (0, 0)),
                  pl.BlockSpec((1, d, tn), lambda i, j: (i, 0, j)),
                  pl.BlockSpec((1, 1, tn), lambda i, j: (i, 0, j))],
        out_specs=pl.BlockSpec((1, 8, tn), lambda i, j: (i, 0, j)),
        compiler_params=_cp("parallel", "parallel"),
        name="adaln",
    )(cc, mod_w, mod_b.reshape(depth, 1, n))
    return out.reshape(depth, 8, 6, d)


def _nmm_plain_kernel(x_ref, g_ref, sh_ref, sc_ref, w_ref, ws_ref, o_ref, os_ref, u_ref, *, tm):
    @pl.when(pl.program_id(2) == 0)
    def _():
        load, _ = _row_access(x_ref, False, 0)
        _norm_mod_rows(load, u_ref, 0, tm, g_ref[...] * (1.0 + sc_ref[...]), sh_ref[...])
        os_ref[...] = _dot(u_ref[...], ws_ref[...])

    o_ref[...] = _dot(u_ref[...], w_ref[...]).astype(o_ref.dtype)


def _nmm_plain(x, g, sh, sc, w, li, n, tm, tn):
    bsz, l, d = x.shape
    ns = w.shape[2] - n
    assert n % tn == 0 and n % ns == 0
    vec = pl.BlockSpec((1, d), lambda b, i, j: (0, 0))
    mod = pl.BlockSpec((None, 1, d), lambda b, i, j: (b, 0, 0))
    return pl.pallas_call(
        functools.partial(_nmm_plain_kernel, tm=tm),
        out_shape=(jax.ShapeDtypeStruct((bsz, l, n), BF16),
                   jax.ShapeDtypeStruct((bsz, l, ns), F32)),
        grid=(bsz, l // tm, n // tn),
        in_specs=[pl.BlockSpec((None, tm, d), lambda b, i, j: (b, i, 0)), vec, mod, mod,
                  pl.BlockSpec((None, d, tn), lambda b, i, j: (li, 0, j)),
                  pl.BlockSpec((None, d, ns), lambda b, i, j: (li, 0, n // ns))],
        out_specs=(pl.BlockSpec((None, tm, tn), lambda b, i, j: (b, i, j)),
                   pl.BlockSpec((None, tm, ns), lambda b, i, j: (b, i, 0))),
        scratch_shapes=[pltpu.VMEM((tm, d), BF16)],
        compiler_params=_cp("parallel", "parallel", "arbitrary"),
        name="norm_mod_matmul",
    )(x, g.reshape(1, d), sh, sc, w, w)


def _nmm_glu_kernel(x_ref, g_ref, sh_ref, sc_ref, wa_ref, wg_ref, ba_ref, bg_ref, o_ref, u_ref,
                    *, tm):
    @pl.when(pl.program_id(2) == 0)
    def _():
        load, _ = _row_access(x_ref, False, 0)
        _norm_mod_rows(load, u_ref, 0, tm, g_ref[...] * (1.0 + sc_ref[...]), sh_ref[...])

    a = _dot(u_ref[...], wa_ref[...]) + ba_ref[...]
    gt = _dot(u_ref[...], wg_ref[...]) + bg_ref[...]
    o_ref[...] = (a * _sigmoid(gt)).astype(o_ref.dtype)


def _nmm_glu(x, g, sh, sc, w, li, bias, tm, tn):
    bsz, l, d = x.shape
    n = w.shape[2] // 2
    nj = n // tn
    vec = pl.BlockSpec((1, d), lambda b, i, j: (0, 0))
    mod = pl.BlockSpec((None, 1, d), lambda b, i, j: (b, 0, 0))
    b2 = bias.reshape(1, 2 * n)
    return pl.pallas_call(
        functools.partial(_nmm_glu_kernel, tm=tm),
        out_shape=jax.ShapeDtypeStruct((bsz, l, n), BF16),
        grid=(bsz, l // tm, nj),
        in_specs=[pl.BlockSpec((None, tm, d), lambda b, i, j: (b, i, 0)), vec, mod, mod,
                  pl.BlockSpec((None, d, tn), lambda b, i, j: (li, 0, j)),
                  pl.BlockSpec((None, d, tn), lambda b, i, j: (li, 0, j + nj)),
                  pl.BlockSpec((1, tn), lambda b, i, j: (0, j)),
                  pl.BlockSpec((1, tn), lambda b, i, j: (0, j + nj))],
        out_specs=pl.BlockSpec((None, tm, tn), lambda b, i, j: (b, i, j)),
        scratch_shapes=[pltpu.VMEM((tm, d), BF16)],
        compiler_params=_cp("parallel", "parallel", "arbitrary"),
        name="norm_mod_glu",
    )(x, g.reshape(1, d), sh, sc, w, w, b2, b2)


def _post_kernel(y_ref, w_ref, b_ref, pg_ref, gate_ref, h_ref, o_ref, acc_ref, *, tm):
    k = pl.program_id(2)
    part = _dot(y_ref[...], w_ref[...])

    @pl.when(k == 0)
    def _():
        acc_ref[...] = part

    @pl.when(k > 0)
    def _():
        acc_ref[...] += part

    @pl.when(k == pl.num_programs(2) - 1)
    def _():
        load_h, _ = _row_access(h_ref, False, 0)
        _, store_o = _row_access(o_ref, False, 0)
        _post_rows(acc_ref, 0, tm, b_ref[...], pg_ref[...] * gate_ref[...], load_h, store_o)


def _post(y, w, li, bias, pg, gate, h, tm, tk):
    bsz, l, d = h.shape
    kdim = y.shape[2]
    vec = pl.BlockSpec((1, d), lambda b, i, k: (0, 0))
    mod = pl.BlockSpec((None, 1, d), lambda b, i, k: (b, 0, 0))
    hspec = pl.BlockSpec((None, tm, d), lambda b, i, k: (b, i, 0))
    return pl.pallas_call(
        functools.partial(_post_kernel, tm=tm),
        out_shape=jax.ShapeDtypeStruct(h.shape, F32),
        grid=(bsz, l // tm, kdim // tk),
        in_specs=[pl.BlockSpec((None, tm, tk), lambda b, i, k: (b, i, k)),
                  pl.BlockSpec((None, tk, d), lambda b, i, k: (li, k, 0)),
                  vec, vec, mod, hspec],
        out_specs=hspec,
        scratch_shapes=[pltpu.VMEM((tm, d), F32)],
        compiler_params=_cp("parallel", "parallel", "arbitrary"),
        name="matmul_post_norm_residual",
    )(y, w, bias.reshape(1, d), pg.reshape(1, d), gate, h)


def _mlp_kernel(x_ref, g_ref, sh_ref, sc_ref, w1_ref, w2_ref, pg_ref, gate_ref,
                o_ref, u_ref, acc_ref, *maybe_xtok_ref, tm, in_t, out_t):
    k = pl.program_id(2)
    segs = [(s, GRID_W) for s in range(tm // GRID_W)] if out_t else [(0, tm)]
    if in_t:
        xtok_ref, = maybe_xtok_ref
    else:
        xtok_ref = x_ref

    @pl.when(k == 0)
    def _():
        if in_t:
            for s in range(tm // GRID_W):
                for r in range(0, GRID_W, F32_ROWS):
                    xtok_ref[s * GRID_W + r:s * GRID_W + r + F32_ROWS, :] = (
                        x_ref[r:r + F32_ROWS, s, :])
        mul = g_ref[...] * (1.0 + sc_ref[...])
        sh = sh_ref[...]
        for s, n in segs:
            load, _ = _row_access(xtok_ref, False, s)
            _norm_mod_rows(load, u_ref, s * GRID_W, n, mul, sh)

    hid = jnp.square(jnp.maximum(_dot(u_ref[...], w1_ref[...]), 0.0)).astype(BF16)
    part = _dot(hid, w2_ref[...])

    @pl.when(k == 0)
    def _():
        acc_ref[...] = part

    @pl.when(k > 0)
    def _():
        acc_ref[...] += part

    @pl.when(k == pl.num_programs(2) - 1)
    def _():
        mul = pg_ref[...] * gate_ref[...]
        for s, n in segs:
            load_h, _ = _row_access(xtok_ref, False, s)
            _, store_o = _row_access(o_ref, out_t, s)
            _post_rows(acc_ref, s * GRID_W, n, 0.0, mul, load_h, store_o)


def _mlp(h, g, sh, sc, w1, w2, li, pg, gate, tm, tf, in_t=False, out_t=False):
    bsz, l, d = h.shape
    ff = w1.shape[2]
    rows = l // GRID_W
    vec = pl.BlockSpec((1, d), lambda b, i, k: (0, 0))
    mod = pl.BlockSpec((None, 1, d), lambda b, i, k: (b, 0, 0))
    plain = pl.BlockSpec((None, tm, d), lambda b, i, k: (b, i, 0))
    if in_t or out_t:
        assert tm % GRID_W == 0 and (tm // GRID_W) % F32_ROWS == 0 and not (in_t and out_t)
    grid_view = pl.BlockSpec((None, GRID_W, tm // GRID_W, d), lambda b, i, k: (b, 0, i, 0))
    xin = h.reshape(bsz, GRID_W, rows, d) if in_t else h
    oshape = (bsz, GRID_W, rows, d) if out_t else (bsz, l, d)
    out = pl.pallas_call(
        functools.partial(_mlp_kernel, tm=tm, in_t=in_t, out_t=out_t),
        out_shape=jax.ShapeDtypeStruct(oshape, F32),
        grid=(bsz, l // tm, ff // tf),
        in_specs=[grid_view if in_t else plain, vec, mod, mod,
                  pl.BlockSpec((None, d, tf), lambda b, i, k: (li, 0, k)),
                  pl.BlockSpec((None, tf, d), lambda b, i, k: (li, k, 0)),
                  vec, mod],
        out_specs=grid_view if out_t else plain,
        scratch_shapes=[pltpu.VMEM((tm, d), BF16), pltpu.VMEM((tm, d), F32)]
        + ([pltpu.VMEM((tm, d), F32)] if in_t else []),
        compiler_params=_cp("parallel", "parallel", "arbitrary"),
        name="mlp_sq_relu",
    )(xin, g.reshape(1, d), sh, sc, w1, w2, pg.reshape(1, d), gate)
    return out.reshape(bsz, l, d)


def _shift_matrix(tb, kw, ext, pad, seg):
    t = jnp.arange(tb, dtype=jnp.int32)
    k = jnp.arange(kw, dtype=jnp.int32)
    e = jnp.arange(ext, dtype=jnp.int32) - pad
    src = t[:, None, None] + k[None, :, None] - kw // 2
    hit = src == e[None, None, :]
    if seg is not None:
        hit = hit & (src // seg == t[:, None, None] // seg)
    return hit.reshape(tb, kw * ext).astype(BF16)


def _scale_rows_by_taps(x, w_ref, xw_ref, row0, ext, kw):
    n, ch = x.shape
    x3 = x.reshape(n // BF16_ROWS, BF16_ROWS, ch)
    for k in range(kw):
        xw_ref[k * ext + row0:k * ext + row0 + n, :] = (x3 * w_ref[k][None]).reshape(n, ch)


def _ssm_conv_kernel(cur_ref, prev_ref, next_ref, s_ref, w_ref, b_ref, o_ref, xw_ref,
                     *, tl, tb, kw):
    i = pl.program_id(1)
    halo = BF16_ROWS
    ext = tb + 2 * halo
    rc = 32
    nblk = tl // tb
    keep_prev = jnp.where(i > 0, 1.0, 0.0).astype(BF16)
    keep_next = jnp.where(i < pl.num_programs(1) - 1, 1.0, 0.0).astype(BF16)
    for blk in range(nblk):
        t0 = blk * tb
        if blk == 0:
            before = prev_ref[...] * keep_prev
        else:
            before = cur_ref[t0 - halo:t0, :]
        if blk == nblk - 1:
            after = next_ref[...] * keep_next
        else:
            after = cur_ref[t0 + tb:t0 + tb + halo, :]
        _scale_rows_by_taps(before, w_ref, xw_ref, 0, ext, kw)
        for r in range(0, tb, rc):
            _scale_rows_by_taps(cur_ref[t0 + r:t0 + r + rc, :], w_ref, xw_ref,
                                halo + r, ext, kw)
        _scale_rows_by_taps(after, w_ref, xw_ref, halo + tb, ext, kw)
        acc = _dot(s_ref[...], xw_ref[...]) + b_ref[...]
        o_ref[t0:t0 + tb, :] = (acc * _sigmoid(acc)).astype(o_ref.dtype)


def _ssm_conv(zx, conv_w, conv_b, tl, tc):
    bsz, l, _ = zx.shape
    kw, ch = conv_w.shape
    tb = 128
    ext = tb + 2 * BF16_ROWS
    off = D_INNER // tc
    hb = tl // BF16_ROWS
    nhb = l // BF16_ROWS
    return pl.pallas_call(
        functools.partial(_ssm_conv_kernel, tl=tl, tb=tb, kw=kw),
        out_shape=jax.ShapeDtypeStruct((bsz, l, ch), BF16),
        grid=(bsz, l // tl, ch // tc),
        in_specs=[pl.BlockSpec((None, tl, tc), lambda b, i, j: (b, i, j + off)),
                  pl.BlockSpec((None, BF16_ROWS, tc),
                               lambda b, i, j: (b, jnp.maximum(i * hb - 1, 0), j + off)),
                  pl.BlockSpec((None, BF16_ROWS, tc),
                               lambda b, i, j: (b, jnp.minimum((i + 1) * hb, nhb - 1), j + off)),
                  pl.BlockSpec((tb, kw * ext), lambda b, i, j: (0, 0)),
                  pl.BlockSpec((kw, BF16_ROWS, tc), lambda b, i, j: (0, 0, j)),
                  pl.BlockSpec((1, tc), lambda b, i, j: (0, j))],
        out_specs=pl.BlockSpec((None, tl, tc), lambda b, i, j: (b, i, j)),
        scratch_shapes=[pltpu.VMEM((kw * ext, tc), BF16)],
        compiler_params=_cp("parallel", "parallel", "parallel"),
        name="ssm_conv_silu",
    )(zx, zx, zx, _shift_matrix(tb, kw, ext, BF16_ROWS, None),
      jnp.broadcast_to(conv_w.astype(BF16)[:, None, :], (kw, BF16_ROWS, ch)),
      conv_b.reshape(1, ch))


def _split2(v):
    hi = v.astype(BF16)
    return hi, (v - hi.astype(F32)).astype(BF16)


def _ssd_kernel(*refs, reverse, gated, nsub):
    if gated:
        (xbc_ref, dt_ref, alog_ref, dtb_ref, dsk_ref, e2_ref, h0_ref, yf_ref, z_ref, ng_ref,
         y_ref, hf_ref, s_ref, ybuf_ref) = refs
    else:
        (xbc_ref, dt_ref, alog_ref, dtb_ref, dsk_ref, e2_ref, h0_ref,
         y_ref, hf_ref, s_ref) = refs
        ybuf_ref = y_ref
    c = pl.program_id(1)
    q = SSD_CHUNK
    hp = SSM_HEAD_DIM
    gw = (SSM_HEADS // SSM_GROUPS) * hp
    lane0 = SSM_HEADS if reverse else 0
    end = 0 if reverse else q - 1

    @pl.when(c == 0)
    def _():
        s_ref[...] = h0_ref[...]

    ri = lax.broadcasted_iota(jnp.int32, (q, q), 0)
    ci = lax.broadcasted_iota(jnp.int32, (q, q), 1)
    tri = (ci >= ri) if reverse else (ci <= ri)
    tmat = jnp.where(tri, 1.0, 0.0).astype(BF16)
    e2 = e2_ref[...]
    lane = lax.broadcasted_iota(jnp.int32, (q, 2 * hp), 1)

    def expand(v):
        hi, lo = _split2(v)
        return _dot(jnp.concatenate([hi, lo], axis=1), e2)

    for sub in (reversed(range(nsub)) if reverse else range(nsub)):
        rows = slice(sub * q, (sub + 1) * q)
        dt = _softplus(dt_ref[rows, :] + dtb_ref[...])
        dta = dt * (-jnp.exp(alog_ref[...]))
        p0 = dta.astype(BF16)
        r0 = dta - p0.astype(F32)
        p1 = r0.astype(BF16)
        p2 = (r0 - p1.astype(F32)).astype(BF16)
        cs = _dot(tmat, p0) + _dot(tmat, p1) + _dot(tmat, p2)
        cs_t = cs.T
        cs_end = cs[end:end + 1, :]
        dt_x = expand(dt)
        ecs_x = expand(jnp.exp(cs))
        dte_x = expand(jnp.exp(cs_end - cs))

        for g in range(SSM_GROUPS):
            lo, hi = g * gw, (g + 1) * gw
            bg = xbc_ref[rows, D_INNER + g * SSM_STATE:D_INNER + (g + 1) * SSM_STATE]
            cg = xbc_ref[rows, D_INNER + GN + g * SSM_STATE:D_INNER + GN + (g + 1) * SSM_STATE]
            scores = lax.dot_general(cg, bg, (((1,), (1,)), ((), ())),
                                     preferred_element_type=F32)
            xs_g = xbc_ref[rows, lo:hi].astype(F32)
            xdt_g = xs_g * dt_x[:, lo:hi]
            xdt_b = xdt_g.astype(BF16)
            s_g = s_ref[:, lo:hi]
            y_g = _dot(cg, s_g.astype(BF16)) * ecs_x[:, lo:hi] + dsk_ref[:, lo:hi] * xs_g
            s_ref[:, lo:hi] = ecs_x[end:end + 1, lo:hi] * s_g + lax.dot_general(
                bg, (xdt_g * dte_x[:, lo:hi]).astype(BF16), (((0,), (0,)), ((), ())),
                preferred_element_type=F32)
            for kp in range(gw // (2 * hp)):
                gmats = []
                for hh in range(2):
                    hd = lane0 + g * (gw // hp) + 2 * kp + hh
                    seg = cs[:, hd:hd + 1] - cs_t[hd:hd + 1, :]
                    decay = jnp.exp(jnp.where(tri, seg, -jnp.inf))
                    gmats.append((scores * decay).astype(BF16))
                xp = xdt_b[:, kp * 2 * hp:(kp + 1) * 2 * hp]
                rhs = jnp.concatenate([jnp.where(lane < hp, xp, jnp.zeros_like(xp)),
                                       jnp.where(lane >= hp, xp, jnp.zeros_like(xp))], axis=0)
                yd = _dot(jnp.concatenate(gmats, axis=1), rhs)
                cols = slice(lo + kp * 2 * hp, lo + (kp + 1) * 2 * hp)
                y_pair = yd + y_g[:, kp * 2 * hp:(kp + 1) * 2 * hp]
                if gated:
                    zz = z_ref[rows, cols].astype(F32)
                    y_pair = (y_pair + yf_ref[rows, cols]) * (zz * _sigmoid(zz))
                ybuf_ref[rows, cols] = y_pair

    @pl.when(c == pl.num_programs(1) - 1)
    def _():
        hf_ref[...] = s_ref[...]

    if gated:
        load, _ = _row_access(ybuf_ref, False, 0)
        _norm_mod_rows(load, y_ref, 0, nsub * q, ng_ref[...], 0.0)


def _ssd(xa, dt, a_log, dt_bias, d_skip, e2, h0, reverse, gate_args=None):
    bsz, l, _ = xa.shape
    nsub = 2
    rows = nsub * SSD_CHUNK
    nc = l // rows
    gated = gate_args is not None
    cmap = (lambda b, c: (b, nc - 1 - c, 0)) if reverse else (lambda b, c: (b, c, 0))
    const = lambda b, c: (0, 0)
    state = pl.BlockSpec((None, SSM_STATE, D_INNER), lambda b, c: (b, 0, 0))
    tok = pl.BlockSpec((None, rows, D_INNER), cmap)
    in_specs = [pl.BlockSpec((None, rows, XBC), cmap),
                pl.BlockSpec((None, rows, 2 * SSM_HEADS), cmap),
                pl.BlockSpec((1, 2 * SSM_HEADS), const),
                pl.BlockSpec((1, 2 * SSM_HEADS), const),
                pl.BlockSpec((1, D_INNER), const),
                pl.BlockSpec((4 * SSM_HEADS, D_INNER), const),
                state]
    args = [xa, dt, a_log, dt_bias, d_skip, e2, h0]
    scratch = [pltpu.VMEM((SSM_STATE, D_INNER), F32)]
    if gated:
        y_fwd, zx, norm_g = gate_args
        in_specs += [tok, tok, pl.BlockSpec((1, D_INNER), const)]
        args += [y_fwd, zx, norm_g.reshape(1, D_INNER)]
        scratch.append(pltpu.VMEM((rows, D_INNER), F32))
    return pl.pallas_call(
        functools.partial(_ssd_kernel, reverse=reverse, gated=gated, nsub=nsub),
        out_shape=(jax.ShapeDtypeStruct((bsz, l, D_INNER), BF16 if gated else F32),
                   jax.ShapeDtypeStruct((bsz, SSM_STATE, D_INNER), F32)),
        grid=(bsz, nc),
        in_specs=in_specs,
        out_specs=(tok, state),
        scratch_shapes=scratch,
        compiler_params=_cp("parallel", "arbitrary"),
        name="ssd_bwd_gate_norm" if gated else "ssd_fwd",
    )(*args)


def _conf_conv_kernel(v_ref, s_ref, w_ref, b_ref, lg_ref, lb_ref, o_ref, xw_ref, cv_ref,
                      *, tl, tb, kw, lc):
    j = pl.program_id(2)
    rc = 32
    for blk in range(tl // tb):
        for r in range(0, tb, rc):
            _scale_rows_by_taps(v_ref[blk * tb + r:blk * tb + r + rc, :],
                                w_ref, xw_ref, r, tb, kw)
        cv_ref[j, blk * tb:(blk + 1) * tb, :] = _dot(s_ref[...], xw_ref[...]) + b_ref[...]

    @pl.when(j == pl.num_programs(2) - 1)
    def _():
        nj = cv_ref.shape[0]
        inv_d = 1.0 / (nj * lc)
        groups = min(MAX_ROW_GROUPS, tl // F32_ROWS)
        step = F32_ROWS * groups

        def body(i, carry):
            r0 = pl.multiple_of(i * step, step)
            stats = []
            for s in range(groups):
                rows = pl.ds(r0 + F32_ROWS * s, F32_ROWS)
                tot = cv_ref[0, rows, :]
                for c in range(1, nj):
                    tot = tot + cv_ref[c, rows, :]
                mu = jnp.sum(tot, axis=-1, keepdims=True) * inv_d
                sq = (cv_ref[0, rows, :] - mu) * (cv_ref[0, rows, :] - mu)
                for c in range(1, nj):
                    sq = sq + (cv_ref[c, rows, :] - mu) * (cv_ref[c, rows, :] - mu)
                stats.append((mu, lax.rsqrt(jnp.sum(sq, axis=-1, keepdims=True) * inv_d + EPS)))
            for s in range(0, groups, 2):
                for c in range(nj):
                    halves = []
                    for t in (s, s + 1):
                        mu, rstd = stats[t]
                        xc = cv_ref[c, pl.ds(r0 + F32_ROWS * t, F32_ROWS), :] - mu
                        halves.append(xc * rstd * lg_ref[:, c * lc:(c + 1) * lc]
                                      + lb_ref[:, c * lc:(c + 1) * lc])
                    y = jnp.concatenate(halves, axis=0)
                    o_ref[pl.ds(r0 + F32_ROWS * s, BF16_ROWS), c * lc:(c + 1) * lc] = (
                        y * _sigmoid(y)).astype(BF16)
            return carry

        lax.fori_loop(0, tl // step, body, 0)


def _conf_conv(v, dw_w, dw_b, ln_g, ln_b, seg, tl, tb, lc):
    bsz, l, d = v.shape
    kw = dw_w.shape[0]
    vec = pl.BlockSpec((1, d), lambda b, i, j: (0, 0))
    return pl.pallas_call(
        functools.partial(_conf_conv_kernel, tl=tl, tb=tb, kw=kw, lc=lc),
        out_shape=jax.ShapeDtypeStruct((bsz, l, d), BF16),
        grid=(bsz, l // tl, d // lc),
        in_specs=[pl.BlockSpec((None, tl, lc), lambda b, i, j: (b, i, j)),
                  pl.BlockSpec((tb, kw * tb), lambda b, i, j: (0, 0)),
                  pl.BlockSpec((kw, BF16_ROWS, lc), lambda b, i, j: (0, 0, j)),
                  pl.BlockSpec((1, lc), lambda b, i, j: (0, j)),
                  vec, vec],
        out_specs=pl.BlockSpec((None, tl, d), lambda b, i, j: (b, i, 0)),
        scratch_shapes=[pltpu.VMEM((kw * tb, lc), BF16), pltpu.VMEM((d // lc, tl, lc), F32)],
        compiler_params=_cp("parallel", "parallel", "arbitrary"),
        name="conf_conv_ln_swish",
    )(v, _shift_matrix(tb, kw, tb, 0, seg),
      jnp.broadcast_to(dw_w.astype(BF16)[:, None, :], (kw, BF16_ROWS, d)),
      dw_b.reshape(1, d), ln_g.reshape(1, d), ln_b.reshape(1, d))


def _ssm_stream(h, mods, pre_g, post_g, li, w_in, conv_w, conv_b, ssd_f, ssd_b, e2f, e2b,
                norm_g, w_out, h0f, h0b, tm):
    sh1, sc1, g1 = mods
    d = h.shape[2]
    zx, dt = _nmm_plain(h, pre_g, sh1, sc1, w_in, li, ZX, tm, 2048)
    xa = _ssm_conv(zx, conv_w, conv_b, tm, 1024)
    y_f, s_f = _ssd(xa, dt, *ssd_f, e2f, h0f, reverse=False)
    y, s_b = _ssd(xa, dt, *ssd_b, e2b, h0b, reverse=True, gate_args=(y_f, zx, norm_g))
    h = _post(y, w_out, li, jnp.zeros((d,), F32), post_g, g1, h, tm, 2048)
    return h, s_f, s_b


def _conf_stream(h, mods, pre_g, post_g, li, pw1_w, pw1_b, dw_w, dw_b, ln_g, ln_b, pw2_w, pw2_b,
                 seg, tm):
    sh1, sc1, g1 = mods
    tb = max(seg, 128)
    v = _nmm_glu(h, pre_g, sh1, sc1, pw1_w, li, pw1_b, tm, 1024)
    u = _conf_conv(v, dw_w, dw_b, ln_g, ln_b, seg, 256, tb, 1024)
    return _post(u, pw2_w, li, pw2_b, post_g, g1, h, tm, pw2_w.shape[1])


def kernel(x, c, ctx, c_ctx, mod_w, mod_b, pre_mix_g, post_mix_g, pre_mlp_g, post_mlp_g, mlp_w1, mlp_w2, ssm_in_w, ssm_conv_w, ssm_conv_b, ssm_a_log_f, ssm_dt_bias_f, ssm_d_f, ssm_a_log_b, ssm_dt_bias_b, ssm_d_b, ssm_norm_g, ssm_out_w, conf_pw1_w, conf_pw1_b, conf_dw_w, conf_dw_b, conf_ln_g, conf_ln_b, conf_pw2_w, conf_pw2_b):
    bsz, seq_len, d = x.shape
    ctx_len = ctx.shape[1]
    depth = mod_w.shape[0]
    rows = seq_len // GRID_W
    tm, tmc = 512, ctx_len
    mods = _adaln(c, c_ctx, mod_w, mod_b)
    w1, w2 = mlp_w1.astype(BF16), mlp_w2.astype(BF16)
    ssm_in_b, ssm_out_b = ssm_in_w.astype(BF16), ssm_out_w.astype(BF16)
    conf_pw1_b16, conf_pw2_b16 = conf_pw1_w.astype(BF16), conf_pw2_w.astype(BF16)

    head_of_chan = jnp.arange(D_INNER, dtype=jnp.int32) // SSM_HEAD_DIM
    lane_head = jnp.arange(2 * SSM_HEADS, dtype=jnp.int32)
    e_f = (lane_head[:, None] == head_of_chan[None, :]).astype(BF16)
    e_b = (lane_head[:, None] == head_of_chan[None, :] + SSM_HEADS).astype(BF16)
    e2f = jnp.concatenate([e_f, e_f], axis=0)
    e2b = jnp.concatenate([e_b, e_b], axis=0)

    def is_col_major(i):
        return ((i // 2) % 2) == 1

    h, hc = x, ctx
    for i in range(depth):
        last = i == depth - 1
        kind = i % 2
        j = i // 2
        col_major = is_col_major(i)
        lat = [mods[i, :bsz, k][:, None, :] for k in range(6)]
        cm = [jnp.broadcast_to(mods[i, bsz, k][None, None, :], (bsz, 1, d)) for k in range(6)]
        if kind == 0:
            a_log = jnp.concatenate([ssm_a_log_f[j], ssm_a_log_b[j]]).reshape(1, -1)
            dt_bias = jnp.concatenate([ssm_dt_bias_f[j], ssm_dt_bias_b[j]]).reshape(1, -1)
            ssd_f = (a_log, dt_bias, jnp.repeat(ssm_d_f[j], SSM_HEAD_DIM).reshape(1, -1))
            ssd_b = (a_log, dt_bias, jnp.repeat(ssm_d_b[j], SSM_HEAD_DIM).reshape(1, -1))
            common = (j, ssm_in_b, ssm_conv_w[j], ssm_conv_b[j], ssd_f, ssd_b, e2f, e2b,
                      ssm_norm_g[j], ssm_out_b)
            zeros = jnp.zeros((bsz, SSM_STATE, D_INNER), F32)
            hc_new, s_f, s_b = _ssm_stream(hc, cm[:3], pre_mix_g[i], post_mix_g[i], *common,
                                           zeros, zeros, tmc)
            h, _, _ = _ssm_stream(h, lat[:3], pre_mix_g[i], post_mix_g[i], *common,
                                  s_f, s_b, tm)
        else:
            common = (j, conf_pw1_b16, conf_pw1_b[j], conf_dw_w[j], conf_dw_b[j],
                      conf_ln_g[j], conf_ln_b[j], conf_pw2_b16, conf_pw2_b[j])
            seg = rows if col_major else GRID_W
            if not last:
                hc_new = _conf_stream(hc, cm[:3], pre_mix_g[i], post_mix_g[i], *common,
                                      ctx_len, tmc)
            h = _conf_stream(h, lat[:3], pre_mix_g[i], post_mix_g[i], *common, seg, tm)
        next_col_major = False if last else is_col_major(i + 1)
        h = _mlp(h, pre_mlp_g[i], lat[3], lat[4], w1, w2, i, post_mlp_g[i], lat[5], tm, 1024,
                 in_t=col_major and not next_col_major, out_t=next_col_major and not col_major)
        if not last:
            hc = _mlp(hc_new, pre_mlp_g[i], cm[3], cm[4], w1, w2, i, post_mlp_g[i], cm[5], tmc, 1024)
    return h
```

```python
import functools

import jax
import jax.numpy as jnp
from jax import lax
from jax.experimental import pallas as pl
from jax.experimental.pallas import tpu as pltpu

F32 = jnp.float32
BF16 = jnp.bfloat16

EPS = 1e-6
GRID_W = 64
SSM_HEADS = 64
SSM_HEAD_DIM = 64
SSM_GROUPS = 8
SSM_STATE = 128
SSD_CHUNK = 128
D_INNER = SSM_HEADS * SSM_HEAD_DIM
GN = SSM_GROUPS * SSM_STATE
XBC = D_INNER + 2 * GN
ZX = D_INNER + XBC

V7X_VMEM_LIMIT = 56 * 1024 * 1024
F32_ROWS = 8
BF16_ROWS = 16
MAX_ROW_GROUPS = 16


def _cp(*sem):
    return pltpu.CompilerParams(dimension_semantics=sem, vmem_limit_bytes=V7X_VMEM_LIMIT)


def _dot(a, b):
    return jnp.dot(a, b, preferred_element_type=F32)


def _sigmoid(v):
    return 1.0 / (1.0 + jnp.exp(-v))


def _softplus(v):
    return jnp.maximum(v, 0.0) + jnp.log(1.0 + jnp.exp(-jnp.abs(v)))


def _row_access(ref, transposed, seg):
    if transposed:
        def load(r, n):
            return ref[pl.ds(r, n), seg, :]

        def store(r, v):
            ref[pl.ds(r, v.shape[0]), seg, :] = v
    else:
        def load(r, n):
            return ref[pl.ds(seg * GRID_W + r, n), :]

        def store(r, v):
            ref[pl.ds(seg * GRID_W + r, v.shape[0]), :] = v
    return load, store


def _norm_mod_rows(load, u_ref, u_row0, ntok, mul, sh):
    groups = min(MAX_ROW_GROUPS, ntok // F32_ROWS)
    step = F32_ROWS * groups

    def body(i, carry):
        r0 = pl.multiple_of(i * step, step)
        rs = []
        for s in range(groups):
            xs = load(r0 + F32_ROWS * s, F32_ROWS)
            rs.append(lax.rsqrt(jnp.mean(xs * xs, axis=-1, keepdims=True) + EPS))
        for s in range(0, groups, 2):
            ua = load(r0 + F32_ROWS * s, F32_ROWS) * rs[s] * mul + sh
            ub = load(r0 + F32_ROWS * (s + 1), F32_ROWS) * rs[s + 1] * mul + sh
            u_ref[pl.ds(u_row0 + r0 + F32_ROWS * s, BF16_ROWS), :] = (
                jnp.concatenate([ua, ub], axis=0).astype(BF16))
        return carry

    lax.fori_loop(0, ntok // step, body, 0)


def _post_rows(acc_ref, acc_row0, ntok, bias, mul, load_h, store_o):
    groups = min(MAX_ROW_GROUPS, ntok // F32_ROWS)
    step = F32_ROWS * groups

    def body(i, carry):
        r0 = pl.multiple_of(i * step, step)
        rs = []
        for s in range(groups):
            f = acc_ref[pl.ds(acc_row0 + r0 + F32_ROWS * s, F32_ROWS), :] + bias
            rs.append(lax.rsqrt(jnp.mean(f * f, axis=-1, keepdims=True) + EPS))
        for s in range(groups):
            f = acc_ref[pl.ds(acc_row0 + r0 + F32_ROWS * s, F32_ROWS), :] + bias
            store_o(r0 + F32_ROWS * s, load_h(r0 + F32_ROWS * s, F32_ROWS) + f * rs[s] * mul)
        return carry

    lax.fori_loop(0, ntok // step, body, 0)


def _adaln_kernel(c_ref, w_ref, b_ref, o_ref):
    cv = c_ref[...]
    s = cv * _sigmoid(cv)
    o_ref[0] = _dot(s.astype(BF16), w_ref[0].astype(BF16)) + b_ref[0]


def _adaln(c, c_ctx, mod_w, mod_b):
    depth, d, n = mod_w.shape
    bsz = c.shape[0]
    cc = jnp.zeros((8, d), F32).at[:bsz].set(c).at[bsz].set(c_ctx)
    tn = 1024
    out = pl.pallas_call(
        _adaln_kernel,
        out_shape=jax.ShapeDtypeStruct((depth, 8, n), F32),
        grid=(depth, n // tn),
        in_specs=[pl.BlockSpec((8, d), lambda i, j: (0, 0)),
                  pl.BlockSpec((1, d, tn), lambda i, j: (i, 0, j)),
                  pl.BlockSpec((1, 1, tn), lambda i, j: (i, 0, j))],
        out_specs=pl.BlockSpec((1, 8, tn), lambda i, j: (i, 0, j)),
        compiler_params=_cp("parallel", "parallel"),
        name="adaln",
    )(cc, mod_w, mod_b.reshape(depth, 1, n))
    return out.reshape(depth, 8, 6, d)


def _nmm_plain_kernel(x_ref, g_ref, sh_ref, sc_ref, w_ref, ws_ref, o_ref, os_ref, u_ref, *, tm):
    @pl.when(pl.program_id(2) == 0)
    def _():
        load, _ = _row_access(x_ref, False, 0)
        _norm_mod_rows(load, u_ref, 0, tm, g_ref[...] * (1.0 + sc_ref[...]), sh_ref[...])
        os_ref[...] = _dot(u_ref[...], ws_ref[...])

    o_ref[...] = _dot(u_ref[...], w_ref[...]).astype(o_ref.dtype)


def _nmm_plain(x, g, sh, sc, w, li, n, tm, tn):
    bsz, l, d = x.shape
    ns = w.shape[2] - n
    assert n % tn == 0 and n % ns == 0
    vec = pl.BlockSpec((1, d), lambda b, i, j: (0, 0))
    mod = pl.BlockSpec((None, 1, d), lambda b, i, j: (b, 0, 0))
    return pl.pallas_call(
        functools.partial(_nmm_plain_kernel, tm=tm),
        out_shape=(jax.ShapeDtypeStruct((bsz, l, n), BF16),
                   jax.ShapeDtypeStruct((bsz, l, ns), F32)),
        grid=(bsz, l // tm, n // tn),
        in_specs=[pl.BlockSpec((None, tm, d), lambda b, i, j: (b, i, 0)), vec, mod, mod,
                  pl.BlockSpec((None, d, tn), lambda b, i, j: (li, 0, j)),
                  pl.BlockSpec((None, d, ns), lambda b, i, j: (li, 0, n // ns))],
        out_specs=(pl.BlockSpec((None, tm, tn), lambda b, i, j: (b, i, j)),
                   pl.BlockSpec((None, tm, ns), lambda b, i, j: (b, i, 0))),
        scratch_shapes=[pltpu.VMEM((tm, d), BF16)],
        compiler_params=_cp("parallel", "parallel", "arbitrary"),
        name="norm_mod_matmul",
    )(x, g.reshape(1, d), sh, sc, w, w)


def _nmm_glu_kernel(x_ref, g_ref, sh_ref, sc_ref, wa_ref, wg_ref, ba_ref, bg_ref, o_ref, u_ref,
                    *, tm):
    @pl.when(pl.program_id(2) == 0)
    def _():
        load, _ = _row_access(x_ref, False, 0)
        _norm_mod_rows(load, u_ref, 0, tm, g_ref[...] * (1.0 + sc_ref[...]), sh_ref[...])

    a = _dot(u_ref[...], wa_ref[...]) + ba_ref[...]
    gt = _dot(u_ref[...], wg_ref[...]) + bg_ref[...]
    o_ref[...] = (a * _sigmoid(gt)).astype(o_ref.dtype)


def _nmm_glu(x, g, sh, sc, w, li, bias, tm, tn):
    bsz, l, d = x.shape
    n = w.shape[2] // 2
    nj = n // tn
    vec = pl.BlockSpec((1, d), lambda b, i, j: (0, 0))
    mod = pl.BlockSpec((None, 1, d), lambda b, i, j: (b, 0, 0))
    b2 = bias.reshape(1, 2 * n)
    return pl.pallas_call(
        functools.partial(_nmm_glu_kernel, tm=tm),
        out_shape=jax.ShapeDtypeStruct((bsz, l, n), BF16),
        grid=(bsz, l // tm, nj),
        in_specs=[pl.BlockSpec((None, tm, d), lambda b, i, j: (b, i, 0)), vec, mod, mod,
                  pl.BlockSpec((None, d, tn), lambda b, i, j: (li, 0, j)),
                  pl.BlockSpec((None, d, tn), lambda b, i, j: (li, 0, j + nj)),
                  pl.BlockSpec((1, tn), lambda b, i, j: (0, j)),
                  pl.BlockSpec((1, tn), lambda b, i, j: (0, j + nj))],
        out_specs=pl.BlockSpec((None, tm, tn), lambda b, i, j: (b, i, j)),
        scratch_shapes=[pltpu.VMEM((tm, d), BF16)],
        compiler_params=_cp("parallel", "parallel", "arbitrary"),
        name="norm_mod_glu",
    )(x, g.reshape(1, d), sh, sc, w, w, b2, b2)


def _post_kernel(y_ref, w_ref, b_ref, pg_ref, gate_ref, h_ref, o_ref, acc_ref, *, tm):
    k = pl.program_id(2)
    part = _dot(y_ref[...], w_ref[...])

    @pl.when(k == 0)
    def _():
        acc_ref[...] = part

    @pl.when(k > 0)
    def _():
        acc_ref[...] += part

    @pl.when(k == pl.num_programs(2) - 1)
    def _():
        load_h, _ = _row_access(h_ref, False, 0)
        _, store_o = _row_access(o_ref, False, 0)
        _post_rows(acc_ref, 0, tm, b_ref[...], pg_ref[...] * gate_ref[...], load_h, store_o)


def _post(y, w, li, bias, pg, gate, h, tm, tk):
    bsz, l, d = h.shape
    kdim = y.shape[2]
    vec = pl.BlockSpec((1, d), lambda b, i, k: (0, 0))
    mod = pl.BlockSpec((None, 1, d), lambda b, i, k: (b, 0, 0))
    hspec = pl.BlockSpec((None, tm, d), lambda b, i, k: (b, i, 0))
    return pl.pallas_call(
        functools.partial(_post_kernel, tm=tm),
        out_shape=jax.ShapeDtypeStruct(h.shape, F32),
        grid=(bsz, l // tm, kdim // tk),
        in_specs=[pl.BlockSpec((None, tm, tk), lambda b, i, k: (b, i, k)),
                  pl.BlockSpec((None, tk, d), lambda b, i, k: (li, k, 0)),
                  vec, vec, mod, hspec],
        out_specs=hspec,
        scratch_shapes=[pltpu.VMEM((tm, d), F32)],
        compiler_params=_cp("parallel", "parallel", "arbitrary"),
        name="matmul_post_norm_residual",
    )(y, w, bias.reshape(1, d), pg.reshape(1, d), gate, h)


def _mlp_kernel(x_ref, g_ref, sh_ref, sc_ref, w1_ref, w2_ref, pg_ref, gate_ref,
                o_ref, u_ref, acc_ref, *maybe_xtok_ref, tm, in_t, out_t):
    k = pl.program_id(2)
    segs = [(s, GRID_W) for s in range(tm // GRID_W)] if out_t else [(0, tm)]
    if in_t:
        xtok_ref, = maybe_xtok_ref
    else:
        xtok_ref = x_ref

    @pl.when(k == 0)
    def _():
        if in_t:
            for s in range(tm // GRID_W):
                for r in range(0, GRID_W, F32_ROWS):
                    xtok_ref[s * GRID_W + r:s * GRID_W + r + F32_ROWS, :] = (
                        x_ref[r:r + F32_ROWS, s, :])
        mul = g_ref[...] * (1.0 + sc_ref[...])
        sh = sh_ref[...]
        for s, n in segs:
            load, _ = _row_access(xtok_ref, False, s)
            _norm_mod_rows(load, u_ref, s * GRID_W, n, mul, sh)

    hid = jnp.square(jnp.maximum(_dot(u_ref[...], w1_ref[...]), 0.0)).astype(BF16)
    part = _dot(hid, w2_ref[...])

    @pl.when(k == 0)
    def _():
        acc_ref[...] = part

    @pl.when(k > 0)
    def _():
        acc_ref[...] += part

    @pl.when(k == pl.num_programs(2) - 1)
    def _():
        mul = pg_ref[...] * gate_ref[...]
        for s, n in segs:
            load_h, _ = _row_access(xtok_ref, False, s)
            _, store_o = _row_access(o_ref, out_t, s)
            _post_rows(acc_ref, s * GRID_W, n, 0.0, mul, load_h, store_o)


def _mlp(h, g, sh, sc, w1, w2, li, pg, gate, tm, tf, in_t=False, out_t=False):
    bsz, l, d = h.shape
    ff = w1.shape[2]
    rows = l // GRID_W
    vec = pl.BlockSpec((1, d), lambda b, i, k: (0, 0))
    mod = pl.BlockSpec((None, 1, d), lambda b, i, k: (b, 0, 0))
    plain = pl.BlockSpec((None, tm, d), lambda b, i, k: (b, i, 0))
    if in_t or out_t:
        assert tm % GRID_W == 0 and (tm // GRID_W) % F32_ROWS == 0 and not (in_t and out_t)
    grid_view = pl.BlockSpec((None, GRID_W, tm // GRID_W, d), lambda b, i, k: (b, 0, i, 0))
    xin = h.reshape(bsz, GRID_W, rows, d) if in_t else h
    oshape = (bsz, GRID_W, rows, d) if out_t else (bsz, l, d)
    out = pl.pallas_call(
        functools.partial(_mlp_kernel, tm=tm, in_t=in_t, out_t=out_t),
        out_shape=jax.ShapeDtypeStruct(oshape, F32),
        grid=(bsz, l // tm, ff // tf),
        in_specs=[grid_view if in_t else plain, vec, mod, mod,
                  pl.BlockSpec((None, d, tf), lambda b, i, k: (li, 0, k)),
                  pl.BlockSpec((None, tf, d), lambda b, i, k: (li, k, 0)),
                  vec, mod],
        out_specs=grid_view if out_t else plain,
        scratch_shapes=[pltpu.VMEM((tm, d), BF16), pltpu.VMEM((tm, d), F32)]
        + ([pltpu.VMEM((tm, d), F32)] if in_t else []),
        compiler_params=_cp("parallel", "parallel", "arbitrary"),
        name="mlp_sq_relu",
    )(xin, g.reshape(1, d), sh, sc, w1, w2, pg.reshape(1, d), gate)
    return out.reshape(bsz, l, d)


def _shift_matrix(tb, kw, ext, pad, seg):
    t = jnp.arange(tb, dtype=jnp.int32)
    k = jnp.arange(kw, dtype=jnp.int32)
    e = jnp.arange(ext, dtype=jnp.int32) - pad
    src = t[:, None, None] + k[None, :, None] - kw // 2
    hit = src == e[None, None, :]
    if seg is not None:
        hit = hit & (src // seg == t[:, None, None] // seg)
    return hit.reshape(tb, kw * ext).astype(BF16)


def _scale_rows_by_taps(x, w_ref, xw_ref, row0, ext, kw):
    n, ch = x.shape
    x3 = x.reshape(n // BF16_ROWS, BF16_ROWS, ch)
    for k in range(kw):
        xw_ref[k * ext + row0:k * ext + row0 + n, :] = (x3 * w_ref[k][None]).reshape(n, ch)


def _ssm_conv_kernel(cur_ref, prev_ref, next_ref, s_ref, w_ref, b_ref, o_ref, xw_ref,
                     *, tl, tb, kw):
    i = pl.program_id(1)
    halo = BF16_ROWS
    ext = tb + 2 * halo
    rc = 32
    nblk = tl // tb
    keep_prev = jnp.where(i > 0, 1.0, 0.0).astype(BF16)
    keep_next = jnp.where(i < pl.num_programs(1) - 1, 1.0, 0.0).astype(BF16)
    for blk in range(nblk):
        t0 = blk * tb
        if blk == 0:
            before = prev_ref[...] * keep_prev
        else:
            before = cur_ref[t0 - halo:t0, :]
        if blk == nblk - 1:
            after = next_ref[...] * keep_next
        else:
            after = cur_ref[t0 + tb:t0 + tb + halo, :]
        _scale_rows_by_taps(before, w_ref, xw_ref, 0, ext, kw)
        for r in range(0, tb, rc):
            _scale_rows_by_taps(cur_ref[t0 + r:t0 + r + rc, :], w_ref, xw_ref,
                                halo + r, ext, kw)
        _scale_rows_by_taps(after, w_ref, xw_ref, halo + tb, ext, kw)
        acc = _dot(s_ref[...], xw_ref[...]) + b_ref[...]
        o_ref[t0:t0 + tb, :] = (acc * _sigmoid(acc)).astype(o_ref.dtype)


def _ssm_conv(zx, conv_w, conv_b, tl, tc):
    bsz, l, _ = zx.shape
    kw, ch = conv_w.shape
    tb = 128
    ext = tb + 2 * BF16_ROWS
    off = D_INNER // tc
    hb = tl // BF16_ROWS
    nhb = l // BF16_ROWS
    return pl.pallas_call(
        functools.partial(_ssm_conv_kernel, tl=tl, tb=tb, kw=kw),
        out_shape=jax.ShapeDtypeStruct((bsz, l, ch), BF16),
        grid=(bsz, l // tl, ch // tc),
        in_specs=[pl.BlockSpec((None, tl, tc), lambda b, i, j: (b, i, j + off)),
                  pl.BlockSpec((None, BF16_ROWS, tc),
                               lambda b, i, j: (b, jnp.maximum(i * hb - 1, 0), j + off)),
                  pl.BlockSpec((None, BF16_ROWS, tc),
                               lambda b, i, j: (b, jnp.minimum((i + 1) * hb, nhb - 1), j + off)),
                  pl.BlockSpec((tb, kw * ext), lambda b, i, j: (0, 0)),
                  pl.BlockSpec((kw, BF16_ROWS, tc), lambda b, i, j: (0, 0, j)),
                  pl.BlockSpec((1, tc), lambda b, i, j: (0, j))],
        out_specs=pl.BlockSpec((None, tl, tc), lambda b, i, j: (b, i, j)),
        scratch_shapes=[pltpu.VMEM((kw * ext, tc), BF16)],
        compiler_params=_cp("parallel", "parallel", "parallel"),
        name="ssm_conv_silu",
    )(zx, zx, zx, _shift_matrix(tb, kw, ext, BF16_ROWS, None),
      jnp.broadcast_to(conv_w.astype(BF16)[:, None, :], (kw, BF16_ROWS, ch)),
      conv_b.reshape(1, ch))


def _split2(v):
    hi = v.astype(BF16)
    return hi, (v - hi.astype(F32)).astype(BF16)


def _ssd_kernel(*refs, reverse, gated, nsub):
    if gated:
        (xbc_ref, dt_ref, alog_ref, dtb_ref, dsk_ref, e2_ref, h0_ref, yf_ref, z_ref, ng_ref,
         y_ref, hf_ref, s_ref, ybuf_ref) = refs
    else:
        (xbc_ref, dt_ref, alog_ref, dtb_ref, dsk_ref, e2_ref, h0_ref,
         y_ref, hf_ref, s_ref) = refs
        ybuf_ref = y_ref
    c = pl.program_id(1)
    q = SSD_CHUNK
    hp = SSM_HEAD_DIM
    gw = (SSM_HEADS // SSM_GROUPS) * hp
    lane0 = SSM_HEADS if reverse else 0
    end = 0 if reverse else q - 1

    @pl.when(c == 0)
    def _():
        s_ref[...] = h0_ref[...]

    ri = lax.broadcasted_iota(jnp.int32, (q, q), 0)
    ci = lax.broadcasted_iota(jnp.int32, (q, q), 1)
    tri = (ci >= ri) if reverse else (ci <= ri)
    tmat = jnp.where(tri, 1.0, 0.0).astype(BF16)
    e2 = e2_ref[...]
    lane = lax.broadcasted_iota(jnp.int32, (q, 2 * hp), 1)

    def expand(v):
        hi, lo = _split2(v)
        return _dot(jnp.concatenate([hi, lo], axis=1), e2)

    for sub in (reversed(range(nsub)) if reverse else range(nsub)):
        rows = slice(sub * q, (sub + 1) * q)
        dt = _softplus(dt_ref[rows, :] + dtb_ref[...])
        dta = dt * (-jnp.exp(alog_ref[...]))
        p0 = dta.astype(BF16)
        r0 = dta - p0.astype(F32)
        p1 = r0.astype(BF16)
        p2 = (r0 - p1.astype(F32)).astype(BF16)
        cs = _dot(tmat, p0) + _dot(tmat, p1) + _dot(tmat, p2)
        cs_t = cs.T
        cs_end = cs[end:end + 1, :]
        dt_x = expand(dt)
        ecs_x = expand(jnp.exp(cs))
        dte_x = expand(jnp.exp(cs_end - cs))

        for g in range(SSM_GROUPS):
            lo, hi = g * gw, (g + 1) * gw
            bg = xbc_ref[rows, D_INNER + g * SSM_STATE:D_INNER + (g + 1) * SSM_STATE]
            cg = xbc_ref[rows, D_INNER + GN + g * SSM_STATE:D_INNER + GN + (g + 1) * SSM_STATE]
            scores = lax.dot_general(cg, bg, (((1,), (1,)), ((), ())),
                                     preferred_element_type=F32)
            xs_g = xbc_ref[rows, lo:hi].astype(F32)
            xdt_g = xs_g * dt_x[:, lo:hi]
            xdt_b = xdt_g.astype(BF16)
            s_g = s_ref[:, lo:hi]
            y_g = _dot(cg, s_g.astype(BF16)) * ecs_x[:, lo:hi] + dsk_ref[:, lo:hi] * xs_g
            s_ref[:, lo:hi] = ecs_x[end:end + 1, lo:hi] * s_g + lax.dot_general(
                bg, (xdt_g * dte_x[:, lo:hi]).astype(BF16), (((0,), (0,)), ((), ())),
                preferred_element_type=F32)
            for kp in range(gw // (2 * hp)):
                gmats = []
                for hh in range(2):
                    hd = lane0 + g * (gw // hp) + 2 * kp + hh
                    seg = cs[:, hd:hd + 1] - cs_t[hd:hd + 1, :]
                    decay = jnp.exp(jnp.where(tri, seg, -jnp.inf))
                    gmats.append((scores * decay).astype(BF16))
                xp = xdt_b[:, kp * 2 * hp:(kp + 1) * 2 * hp]
                rhs = jnp.concatenate([jnp.where(lane < hp, xp, jnp.zeros_like(xp)),
                                       jnp.where(lane >= hp, xp, jnp.zeros_like(xp))], axis=0)
                yd = _dot(jnp.concatenate(gmats, axis=1), rhs)
                cols = slice(lo + kp * 2 * hp, lo + (kp + 1) * 2 * hp)
                y_pair = yd + y_g[:, kp * 2 * hp:(kp + 1) * 2 * hp]
                if gated:
                    zz = z_ref[rows, cols].astype(F32)
                    y_pair = (y_pair + yf_ref[rows, cols]) * (zz * _sigmoid(zz))
                ybuf_ref[rows, cols] = y_pair

    @pl.when(c == pl.num_programs(1) - 1)
    def _():
        hf_ref[...] = s_ref[...]

    if gated:
        load, _ = _row_access(ybuf_ref, False, 0)
        _norm_mod_rows(load, y_ref, 0, nsub * q, ng_ref[...], 0.0)


def _ssd(xa, dt, a_log, dt_bias, d_skip, e2, h0, reverse, gate_args=None):
    bsz, l, _ = xa.shape
    nsub = 2
    rows = nsub * SSD_CHUNK
    nc = l // rows
    gated = gate_args is not None
    cmap = (lambda b, c: (b, nc - 1 - c, 0)) if reverse else (lambda b, c: (b, c, 0))
    const = lambda b, c: (0, 0)
    state = pl.BlockSpec((None, SSM_STATE, D_INNER), lambda b, c: (b, 0, 0))
    tok = pl.BlockSpec((None, rows, D_INNER), cmap)
    in_specs = [pl.BlockSpec((None, rows, XBC), cmap),
                pl.BlockSpec((None, rows, 2 * SSM_HEADS), cmap),
                pl.BlockSpec((1, 2 * SSM_HEADS), const),
                pl.BlockSpec((1, 2 * SSM_HEADS), const),
                pl.BlockSpec((1, D_INNER), const),
                pl.BlockSpec((4 * SSM_HEADS, D_INNER), const),
                state]
    args = [xa, dt, a_log, dt_bias, d_skip, e2, h0]
    scratch = [pltpu.VMEM((SSM_STATE, D_INNER), F32)]
    if gated:
        y_fwd, zx, norm_g = gate_args
        in_specs += [tok, tok, pl.BlockSpec((1, D_INNER), const)]
        args += [y_fwd, zx, norm_g.reshape(1, D_INNER)]
        scratch.append(pltpu.VMEM((rows, D_INNER), F32))
    return pl.pallas_call(
        functools.partial(_ssd_kernel, reverse=reverse, gated=gated, nsub=nsub),
        out_shape=(jax.ShapeDtypeStruct((bsz, l, D_INNER), BF16 if gated else F32),
                   jax.ShapeDtypeStruct((bsz, SSM_STATE, D_INNER), F32)),
        grid=(bsz, nc),
        in_specs=in_specs,
        out_specs=(tok, state),
        scratch_shapes=scratch,
        compiler_params=_cp("parallel", "arbitrary"),
        name="ssd_bwd_gate_norm" if gated else "ssd_fwd",
    )(*args)


def _conf_conv_kernel(v_ref, s_ref, w_ref, b_ref, lg_ref, lb_ref, o_ref, xw_ref, cv_ref,
                      *, tl, tb, kw, lc):
    j = pl.program_id(2)
    rc = 32
    for blk in range(tl // tb):
        for r in range(0, tb, rc):
            _scale_rows_by_taps(v_ref[blk * tb + r:blk * tb + r + rc, :],
                                w_ref, xw_ref, r, tb, kw)
        cv_ref[j, blk * tb:(blk + 1) * tb, :] = _dot(s_ref[...], xw_ref[...]) + b_ref[...]

    @pl.when(j == pl.num_programs(2) - 1)
    def _():
        nj = cv_ref.shape[0]
        inv_d = 1.0 / (nj * lc)
        groups = min(MAX_ROW_GROUPS, tl // F32_ROWS)
        step = F32_ROWS * groups

        def body(i, carry):
            r0 = pl.multiple_of(i * step, step)
            stats = []
            for s in range(groups):
                rows = pl.ds(r0 + F32_ROWS * s, F32_ROWS)
                tot = cv_ref[0, rows, :]
                for c in range(1, nj):
                    tot = tot + cv_ref[c, rows, :]
                mu = jnp.sum(tot, axis=-1, keepdims=True) * inv_d
                sq = (cv_ref[0, rows, :] - mu) * (cv_ref[0, rows, :] - mu)
                for c in range(1, nj):
                    sq = sq + (cv_ref[c, rows, :] - mu) * (cv_ref[c, rows, :] - mu)
                stats.append((mu, lax.rsqrt(jnp.sum(sq, axis=-1, keepdims=True) * inv_d + EPS)))
            for s in range(0, groups, 2):
                for c in range(nj):
                    halves = []
                    for t in (s, s + 1):
                        mu, rstd = stats[t]
                        xc = cv_ref[c, pl.ds(r0 + F32_ROWS * t, F32_ROWS), :] - mu
                        halves.append(xc * rstd * lg_ref[:, c * lc:(c + 1) * lc]
                                      + lb_ref[:, c * lc:(c + 1) * lc])
                    y = jnp.concatenate(halves, axis=0)
                    o_ref[pl.ds(r0 + F32_ROWS * s, BF16_ROWS), c * lc:(c + 1) * lc] = (
                        y * _sigmoid(y)).astype(BF16)
            return carry

        lax.fori_loop(0, tl // step, body, 0)


def _conf_conv(v, dw_w, dw_b, ln_g, ln_b, seg, tl, tb, lc):
    bsz, l, d = v.shape
    kw = dw_w.shape[0]
    vec = pl.BlockSpec((1, d), lambda b, i, j: (0, 0))
    return pl.pallas_call(
        functools.partial(_conf_conv_kernel, tl=tl, tb=tb, kw=kw, lc=lc),
        out_shape=jax.ShapeDtypeStruct((bsz, l, d), BF16),
        grid=(bsz, l // tl, d // lc),
        in_specs=[pl.BlockSpec((None, tl, lc), lambda b, i, j: (b, i, j)),
                  pl.BlockSpec((tb, kw * tb), lambda b, i, j: (0, 0)),
                  pl.BlockSpec((kw, BF16_ROWS, lc), lambda b, i, j: (0, 0, j)),
                  pl.BlockSpec((1, lc), lambda b, i, j: (0, j)),
                  vec, vec],
        out_specs=pl.BlockSpec((None, tl, d), lambda b, i, j: (b, i, 0)),
        scratch_shapes=[pltpu.VMEM((kw * tb, lc), BF16), pltpu.VMEM((d // lc, tl, lc), F32)],
        compiler_params=_cp("parallel", "parallel", "arbitrary"),
        name="conf_conv_ln_swish",
    )(v, _shift_matrix(tb, kw, tb, 0, seg),
      jnp.broadcast_to(dw_w.astype(BF16)[:, None, :], (kw, BF16_ROWS, d)),
      dw_b.reshape(1, d), ln_g.reshape(1, d), ln_b.reshape(1, d))


def _ssm_stream(h, mods, pre_g, post_g, li, w_in, conv_w, conv_b, ssd_f, ssd_b, e2f, e2b,
                norm_g, w_out, h0f, h0b, tm, nseq=1):
    sh1, sc1, g1 = mods
    bflat, lflat, d = h.shape
    lseq = lflat // nseq
    zx, dt = _nmm_plain(h, pre_g, sh1, sc1, w_in, li, ZX, tm, 2048)
    zx, dt = zx.reshape(bflat * nseq, lseq, ZX), dt.reshape(bflat * nseq, lseq, -1)
    xa = _ssm_conv(zx, conv_w, conv_b, min(tm, lseq), 1024)
    y_f, s_f = _ssd(xa, dt, *ssd_f, e2f, h0f, reverse=False)
    y, s_b = _ssd(xa, dt, *ssd_b, e2b, h0b, reverse=True, gate_args=(y_f, zx, norm_g))
    h = _post(y.reshape(bflat, lflat, D_INNER), w_out, li, jnp.zeros((d,), F32), post_g, g1, h,
              tm, 2048)
    return h, s_f, s_b


def _conf_stream(h, mods, pre_g, post_g, li, pw1_w, pw1_b, dw_w, dw_b, ln_g, ln_b, pw2_w, pw2_b,
                 seg, tm):
    sh1, sc1, g1 = mods
    tb = max(seg, 128)
    v = _nmm_glu(h, pre_g, sh1, sc1, pw1_w, li, pw1_b, tm, 1024)
    u = _conf_conv(v, dw_w, dw_b, ln_g, ln_b, seg, 256, tb, 1024)
    return _post(u, pw2_w, li, pw2_b, post_g, g1, h, tm, pw2_w.shape[1])


def kernel(x, c, ctx, c_ctx, mod_w, mod_b, pre_mix_g, post_mix_g, pre_mlp_g, post_mlp_g, mlp_w1, mlp_w2, ssm_in_w, ssm_conv_w, ssm_conv_b, ssm_a_log_f, ssm_dt_bias_f, ssm_d_f, ssm_a_log_b, ssm_dt_bias_b, ssm_d_b, ssm_norm_g, ssm_out_w, conf_pw1_w, conf_pw1_b, conf_dw_w, conf_dw_b, conf_ln_g, conf_ln_b, conf_pw2_w, conf_pw2_b):
    bsz, seq_len, d = x.shape
    ctx_len = ctx.shape[1]
    depth = mod_w.shape[0]
    rows = seq_len // GRID_W
    tm, tmc = 512, bsz * ctx_len
    mods = _adaln(c, c_ctx, mod_w, mod_b)
    w1, w2 = mlp_w1.astype(BF16), mlp_w2.astype(BF16)
    ssm_in_b, ssm_out_b = ssm_in_w.astype(BF16), ssm_out_w.astype(BF16)
    conf_pw1_b16, conf_pw2_b16 = conf_pw1_w.astype(BF16), conf_pw2_w.astype(BF16)

    head_of_chan = jnp.arange(D_INNER, dtype=jnp.int32) // SSM_HEAD_DIM
    lane_head = jnp.arange(2 * SSM_HEADS, dtype=jnp.int32)
    e_f = (lane_head[:, None] == head_of_chan[None, :]).astype(BF16)
    e_b = (lane_head[:, None] == head_of_chan[None, :] + SSM_HEADS).astype(BF16)
    e2f = jnp.concatenate([e_f, e_f], axis=0)
    e2b = jnp.concatenate([e_b, e_b], axis=0)

    def is_col_major(i):
        return ((i // 2) % 2) == 1

    h = x
    hc = ctx.reshape(1, tmc, d)
    for i in range(depth):
        last = i == depth - 1
        kind = i % 2
        j = i // 2
        col_major = is_col_major(i)
        lat = [mods[i, :bsz, k][:, None, :] for k in range(6)]
        cm = [mods[i, bsz, k][None, None, :] for k in range(6)]
        if kind == 0:
            a_log = jnp.concatenate([ssm_a_log_f[j], ssm_a_log_b[j]]).reshape(1, -1)
            dt_bias = jnp.concatenate([ssm_dt_bias_f[j], ssm_dt_bias_b[j]]).reshape(1, -1)
            ssd_f = (a_log, dt_bias, jnp.repeat(ssm_d_f[j], SSM_HEAD_DIM).reshape(1, -1))
            ssd_b = (a_log, dt_bias, jnp.repeat(ssm_d_b[j], SSM_HEAD_DIM).reshape(1, -1))
            common = (j, ssm_in_b, ssm_conv_w[j], ssm_conv_b[j], ssd_f, ssd_b, e2f, e2b,
                      ssm_norm_g[j], ssm_out_b)
            zeros = jnp.zeros((bsz, SSM_STATE, D_INNER), F32)
            hc_new, s_f, s_b = _ssm_stream(hc, cm[:3], pre_mix_g[i], post_mix_g[i], *common,
                                           zeros, zeros, tmc, nseq=bsz)
            h, _, _ = _ssm_stream(h, lat[:3], pre_mix_g[i], post_mix_g[i], *common,
                                  s_f, s_b, tm)
        else:
            common = (j, conf_pw1_b16, conf_pw1_b[j], conf_dw_w[j], conf_dw_b[j],
                      conf_ln_g[j], conf_ln_b[j], conf_pw2_b16, conf_pw2_b[j])
            seg = rows if col_major else GRID_W
            if not last:
                hc_new = _conf_stream(hc, cm[:3], pre_mix_g[i], post_mix_g[i], *common,
                                      ctx_len, tmc)
            h = _conf_stream(h, lat[:3], pre_mix_g[i], post_mix_g[i], *common, seg, tm)
        next_col_major = False if last else is_col_major(i + 1)
        h = _mlp(h, pre_mlp_g[i], lat[3], lat[4], w1, w2, i, post_mlp_g[i], lat[5], tm, 1024,
                 in_t=col_major and not next_col_major, out_t=next_col_major and not col_major)
        if not last:
            hc = _mlp(hc_new, pre_mlp_g[i], cm[3], cm[4], w1, w2, i, post_mlp_g[i], cm[5], tmc, 1024)
    return h
```

```python
import functools

import jax
import jax.numpy as jnp
from jax import lax
from jax.experimental import pallas as pl
from jax.experimental.pallas import tpu as pltpu

F32 = jnp.float32
BF16 = jnp.bfloat16

EPS = 1e-6
GRID_W = 64
SSM_HEADS = 64
SSM_HEAD_DIM = 64
SSM_GROUPS = 8
SSM_STATE = 128
SSD_CHUNK = 128
SSD_CHUNKS_PER_STEP = 2
D_INNER = SSM_HEADS * SSM_HEAD_DIM
GN = SSM_GROUPS * SSM_STATE
XBC = D_INNER + 2 * GN
ZX = D_INNER + XBC

V7X_VMEM_LIMIT = 56 * 1024 * 1024
F32_ROWS = 8
BF16_ROWS = 16
MAX_ROW_GROUPS = 16


def _cp(*sem):
    return pltpu.CompilerParams(dimension_semantics=sem, vmem_limit_bytes=V7X_VMEM_LIMIT)


def _dot(a, b):
    return jnp.dot(a, b, preferred_element_type=F32)


def _sigmoid(v):
    return 1.0 / (1.0 + jnp.exp(-v))


def _softplus(v):
    return jnp.maximum(v, 0.0) + jnp.log(1.0 + jnp.exp(-jnp.abs(v)))


def _row_access(ref, transposed, seg):
    if transposed:
        def load(r, n):
            return ref[pl.ds(r, n), seg, :]

        def store(r, v):
            ref[pl.ds(r, v.shape[0]), seg, :] = v
    else:
        def load(r, n):
            return ref[pl.ds(seg * GRID_W + r, n), :]

        def store(r, v):
            ref[pl.ds(seg * GRID_W + r, v.shape[0]), :] = v
    return load, store


def _norm_mod_rows(load, u_ref, u_row0, ntok, mul, sh):
    groups = min(MAX_ROW_GROUPS, ntok // F32_ROWS)
    step = F32_ROWS * groups

    def body(i, carry):
        r0 = pl.multiple_of(i * step, step)
        rs = []
        for s in range(groups):
            xs = load(r0 + F32_ROWS * s, F32_ROWS)
            rs.append(lax.rsqrt(jnp.mean(xs * xs, axis=-1, keepdims=True) + EPS))
        for s in range(0, groups, 2):
            ua = load(r0 + F32_ROWS * s, F32_ROWS) * rs[s] * mul + sh
            ub = load(r0 + F32_ROWS * (s + 1), F32_ROWS) * rs[s + 1] * mul + sh
            u_ref[pl.ds(u_row0 + r0 + F32_ROWS * s, BF16_ROWS), :] = (
                jnp.concatenate([ua, ub], axis=0).astype(BF16))
        return carry

    lax.fori_loop(0, ntok // step, body, 0)


def _post_rows(acc_ref, acc_row0, ntok, bias, mul, load_h, store_o):
    groups = min(MAX_ROW_GROUPS, ntok // F32_ROWS)
    step = F32_ROWS * groups

    def body(i, carry):
        r0 = pl.multiple_of(i * step, step)
        rs = []
        for s in range(groups):
            f = acc_ref[pl.ds(acc_row0 + r0 + F32_ROWS * s, F32_ROWS), :] + bias
            rs.append(lax.rsqrt(jnp.mean(f * f, axis=-1, keepdims=True) + EPS))
        for s in range(groups):
            f = acc_ref[pl.ds(acc_row0 + r0 + F32_ROWS * s, F32_ROWS), :] + bias
            store_o(r0 + F32_ROWS * s, load_h(r0 + F32_ROWS * s, F32_ROWS) + f * rs[s] * mul)
        return carry

    lax.fori_loop(0, ntok // step, body, 0)


def _adaln_kernel(c_ref, w_ref, b_ref, o_ref):
    cv = c_ref[...]
    s = cv * _sigmoid(cv)
    o_ref[0] = _dot(s.astype(BF16), w_ref[0].astype(BF16)) + b_ref[0]


def _adaln(c, c_ctx, mod_w, mod_b):
    depth, d, n = mod_w.shape
    bsz = c.shape[0]
    cc = jnp.zeros((8, d), F32).at[:bsz].set(c).at[bsz].set(c_ctx)
    tn = 1024
    out = pl.pallas_call(
        _adaln_kernel,
        out_shape=jax.ShapeDtypeStruct((depth, 8, n), F32),
        grid=(depth, n // tn),
        in_specs=[pl.BlockSpec((8, d), lambda i, j: (0, 0)),
                  pl.BlockSpec((1, d, tn), lambda i, j: (i, 0, j)),
                  pl.BlockSpec((1, 1, tn), lambda i, j: (i, 0, j))],
        out_specs=pl.BlockSpec((1, 8, tn), lambda i, j: (i, 0, j)),
        compiler_params=_cp("parallel", "parallel"),
        name="adaln",
    )(cc, mod_w, mod_b.reshape(depth, 1, n))
    return out.reshape(depth, 8, 6, d)


def _nmm_plain_kernel(x_ref, g_ref, sh_ref, sc_ref, w_ref, ws_ref, o_ref, os_ref, u_ref, *, tm):
    @pl.when(pl.program_id(2) == 0)
    def _():
        load, _ = _row_access(x_ref, False, 0)
        _norm_mod_rows(load, u_ref, 0, tm, g_ref[...] * (1.0 + sc_ref[...]), sh_ref[...])
        os_ref[...] = _dot(u_ref[...], ws_ref[...])

    o_ref[...] = _dot(u_ref[...], w_ref[...]).astype(o_ref.dtype)


def _nmm_plain(x, g, sh, sc, w, li, n, tm, tn):
    bsz, l, d = x.shape
    ns = w.shape[2] - n
    assert n % tn == 0 and n % ns == 0
    vec = pl.BlockSpec((1, d), lambda b, i, j: (0, 0))
    mod = pl.BlockSpec((None, 1, d), lambda b, i, j: (b, 0, 0))
    return pl.pallas_call(
        functools.partial(_nmm_plain_kernel, tm=tm),
        out_shape=(jax.ShapeDtypeStruct((bsz, l, n), BF16),
                   jax.ShapeDtypeStruct((bsz, l, ns), F32)),
        grid=(bsz, l // tm, n // tn),
        in_specs=[pl.BlockSpec((None, tm, d), lambda b, i, j: (b, i, 0)), vec, mod, mod,
                  pl.BlockSpec((None, d, tn), lambda b, i, j: (li, 0, j)),
                  pl.BlockSpec((None, d, ns), lambda b, i, j: (li, 0, n // ns))],
        out_specs=(pl.BlockSpec((None, tm, tn), lambda b, i, j: (b, i, j)),
                   pl.BlockSpec((None, tm, ns), lambda b, i, j: (b, i, 0))),
        scratch_shapes=[pltpu.VMEM((tm, d), BF16)],
        compiler_params=_cp("parallel", "parallel", "arbitrary"),
        name="norm_mod_matmul",
    )(x, g.reshape(1, d), sh, sc, w, w)


def _nmm_glu_kernel(x_ref, g_ref, sh_ref, sc_ref, wa_ref, wg_ref, ba_ref, bg_ref, o_ref, u_ref,
                    *, tm):
    @pl.when(pl.program_id(2) == 0)
    def _():
        load, _ = _row_access(x_ref, False, 0)
        _norm_mod_rows(load, u_ref, 0, tm, g_ref[...] * (1.0 + sc_ref[...]), sh_ref[...])

    a = _dot(u_ref[...], wa_ref[...]) + ba_ref[...]
    gt = _dot(u_ref[...], wg_ref[...]) + bg_ref[...]
    o_ref[...] = (a * _sigmoid(gt)).astype(o_ref.dtype)


def _nmm_glu(x, g, sh, sc, w, li, bias, tm, tn):
    bsz, l, d = x.shape
    n = w.shape[2] // 2
    nj = n // tn
    vec = pl.BlockSpec((1, d), lambda b, i, j: (0, 0))
    mod = pl.BlockSpec((None, 1, d), lambda b, i, j: (b, 0, 0))
    b2 = bias.reshape(1, 2 * n)
    return pl.pallas_call(
        functools.partial(_nmm_glu_kernel, tm=tm),
        out_shape=jax.ShapeDtypeStruct((bsz, l, n), BF16),
        grid=(bsz, l // tm, nj),
        in_specs=[pl.BlockSpec((None, tm, d), lambda b, i, j: (b, i, 0)), vec, mod, mod,
                  pl.BlockSpec((None, d, tn), lambda b, i, j: (li, 0, j)),
                  pl.BlockSpec((None, d, tn), lambda b, i, j: (li, 0, j + nj)),
                  pl.BlockSpec((1, tn), lambda b, i, j: (0, j)),
                  pl.BlockSpec((1, tn), lambda b, i, j: (0, j + nj))],
        out_specs=pl.BlockSpec((None, tm, tn), lambda b, i, j: (b, i, j)),
        scratch_shapes=[pltpu.VMEM((tm, d), BF16)],
        compiler_params=_cp("parallel", "parallel", "arbitrary"),
        name="norm_mod_glu",
    )(x, g.reshape(1, d), sh, sc, w, w, b2, b2)


def _post_kernel(y_ref, w_ref, b_ref, pg_ref, gate_ref, h_ref, o_ref, acc_ref, *, tm):
    k = pl.program_id(2)
    part = _dot(y_ref[...], w_ref[...])

    @pl.when(k == 0)
    def _():
        acc_ref[...] = part

    @pl.when(k > 0)
    def _():
        acc_ref[...] += part

    @pl.when(k == pl.num_programs(2) - 1)
    def _():
        load_h, _ = _row_access(h_ref, False, 0)
        _, store_o = _row_access(o_ref, False, 0)
        _post_rows(acc_ref, 0, tm, b_ref[...], pg_ref[...] * gate_ref[...], load_h, store_o)


def _post(y, w, li, bias, pg, gate, h, tm, tk):
    bsz, l, d = h.shape
    kdim = y.shape[2]
    vec = pl.BlockSpec((1, d), lambda b, i, k: (0, 0))
    mod = pl.BlockSpec((None, 1, d), lambda b, i, k: (b, 0, 0))
    hspec = pl.BlockSpec((None, tm, d), lambda b, i, k: (b, i, 0))
    return pl.pallas_call(
        functools.partial(_post_kernel, tm=tm),
        out_shape=jax.ShapeDtypeStruct(h.shape, F32),
        grid=(bsz, l // tm, kdim // tk),
        in_specs=[pl.BlockSpec((None, tm, tk), lambda b, i, k: (b, i, k)),
                  pl.BlockSpec((None, tk, d), lambda b, i, k: (li, k, 0)),
                  vec, vec, mod, hspec],
        out_specs=hspec,
        scratch_shapes=[pltpu.VMEM((tm, d), F32)],
        compiler_params=_cp("parallel", "parallel", "arbitrary"),
        name="matmul_post_norm_residual",
    )(y, w, bias.reshape(1, d), pg.reshape(1, d), gate, h)


def _mlp_kernel(x_ref, g_ref, sh_ref, sc_ref, w1_ref, w2_ref, pg_ref, gate_ref,
                o_ref, u_ref, acc_ref, *maybe_xtok_ref, tm, in_t, out_t):
    k = pl.program_id(2)
    segs = [(s, GRID_W) for s in range(tm // GRID_W)] if out_t else [(0, tm)]
    if in_t:
        xtok_ref, = maybe_xtok_ref
    else:
        xtok_ref = x_ref

    @pl.when(k == 0)
    def _():
        if in_t:
            for s in range(tm // GRID_W):
                for r in range(0, GRID_W, F32_ROWS):
                    xtok_ref[s * GRID_W + r:s * GRID_W + r + F32_ROWS, :] = (
                        x_ref[r:r + F32_ROWS, s, :])
        mul = g_ref[...] * (1.0 + sc_ref[...])
        sh = sh_ref[...]
        for s, n in segs:
            load, _ = _row_access(xtok_ref, False, s)
            _norm_mod_rows(load, u_ref, s * GRID_W, n, mul, sh)

    hid = jnp.square(jnp.maximum(_dot(u_ref[...], w1_ref[...]), 0.0)).astype(BF16)
    part = _dot(hid, w2_ref[...])

    @pl.when(k == 0)
    def _():
        acc_ref[...] = part

    @pl.when(k > 0)
    def _():
        acc_ref[...] += part

    @pl.when(k == pl.num_programs(2) - 1)
    def _():
        mul = pg_ref[...] * gate_ref[...]
        for s, n in segs:
            load_h, _ = _row_access(xtok_ref, False, s)
            _, store_o = _row_access(o_ref, out_t, s)
            _post_rows(acc_ref, s * GRID_W, n, 0.0, mul, load_h, store_o)


def _mlp(h, g, sh, sc, w1, w2, li, pg, gate, tm, tf, in_t=False, out_t=False):
    bsz, l, d = h.shape
    ff = w1.shape[2]
    rows = l // GRID_W
    vec = pl.BlockSpec((1, d), lambda b, i, k: (0, 0))
    mod = pl.BlockSpec((None, 1, d), lambda b, i, k: (b, 0, 0))
    plain = pl.BlockSpec((None, tm, d), lambda b, i, k: (b, i, 0))
    if in_t or out_t:
        assert tm % GRID_W == 0 and (tm // GRID_W) % F32_ROWS == 0 and not (in_t and out_t)
    grid_view = pl.BlockSpec((None, GRID_W, tm // GRID_W, d), lambda b, i, k: (b, 0, i, 0))
    xin = h.reshape(bsz, GRID_W, rows, d) if in_t else h
    oshape = (bsz, GRID_W, rows, d) if out_t else (bsz, l, d)
    out = pl.pallas_call(
        functools.partial(_mlp_kernel, tm=tm, in_t=in_t, out_t=out_t),
        out_shape=jax.ShapeDtypeStruct(oshape, F32),
        grid=(bsz, l // tm, ff // tf),
        in_specs=[grid_view if in_t else plain, vec, mod, mod,
                  pl.BlockSpec((None, d, tf), lambda b, i, k: (li, 0, k)),
                  pl.BlockSpec((None, tf, d), lambda b, i, k: (li, k, 0)),
                  vec, mod],
        out_specs=grid_view if out_t else plain,
        scratch_shapes=[pltpu.VMEM((tm, d), BF16), pltpu.VMEM((tm, d), F32)]
        + ([pltpu.VMEM((tm, d), F32)] if in_t else []),
        compiler_params=_cp("parallel", "parallel", "arbitrary"),
        name="mlp_sq_relu",
    )(xin, g.reshape(1, d), sh, sc, w1, w2, pg.reshape(1, d), gate)
    return out.reshape(bsz, l, d)


def _shift_matrix(tb, kw, ext, pad, seg):
    t = jnp.arange(tb, dtype=jnp.int32)
    k = jnp.arange(kw, dtype=jnp.int32)
    e = jnp.arange(ext, dtype=jnp.int32) - pad
    src = t[:, None, None] + k[None, :, None] - kw // 2
    hit = src == e[None, None, :]
    if seg is not None:
        hit = hit & (src // seg == t[:, None, None] // seg)
    return hit.reshape(tb, kw * ext).astype(BF16)


def _scale_rows_by_taps(x, w_ref, xw_ref, row0, ext, kw):
    n, ch = x.shape
    x3 = x.reshape(n // BF16_ROWS, BF16_ROWS, ch)
    for k in range(kw):
        xw_ref[k * ext + row0:k * ext + row0 + n, :] = (x3 * w_ref[k][None]).reshape(n, ch)


def _ssm_conv_kernel(cur_ref, prev_ref, next_ref, s_ref, w_ref, b_ref, o_ref, xw_ref,
                     *, tl, tb, kw):
    i = pl.program_id(1)
    halo = BF16_ROWS
    ext = tb + 2 * halo
    rc = 32
    nblk = tl // tb
    keep_prev = jnp.where(i > 0, 1.0, 0.0).astype(BF16)
    keep_next = jnp.where(i < pl.num_programs(1) - 1, 1.0, 0.0).astype(BF16)
    for blk in range(nblk):
        t0 = blk * tb
        if blk == 0:
            before = prev_ref[...] * keep_prev
        else:
            before = cur_ref[t0 - halo:t0, :]
        if blk == nblk - 1:
            after = next_ref[...] * keep_next
        else:
            after = cur_ref[t0 + tb:t0 + tb + halo, :]
        _scale_rows_by_taps(before, w_ref, xw_ref, 0, ext, kw)
        for r in range(0, tb, rc):
            _scale_rows_by_taps(cur_ref[t0 + r:t0 + r + rc, :], w_ref, xw_ref,
                                halo + r, ext, kw)
        _scale_rows_by_taps(after, w_ref, xw_ref, halo + tb, ext, kw)
        acc = _dot(s_ref[...], xw_ref[...]) + b_ref[...]
        o_ref[t0:t0 + tb, :] = (acc * _sigmoid(acc)).astype(o_ref.dtype)


def _ssm_conv(zx, conv_w, conv_b, tl, tc):
    bsz, l, _ = zx.shape
    kw, ch = conv_w.shape
    tb = 128
    ext = tb + 2 * BF16_ROWS
    off = D_INNER // tc
    hb = tl // BF16_ROWS
    nhb = l // BF16_ROWS
    return pl.pallas_call(
        functools.partial(_ssm_conv_kernel, tl=tl, tb=tb, kw=kw),
        out_shape=jax.ShapeDtypeStruct((bsz, l, ch), BF16),
        grid=(bsz, l // tl, ch // tc),
        in_specs=[pl.BlockSpec((None, tl, tc), lambda b, i, j: (b, i, j + off)),
                  pl.BlockSpec((None, BF16_ROWS, tc),
                               lambda b, i, j: (b, jnp.maximum(i * hb - 1, 0), j + off)),
                  pl.BlockSpec((None, BF16_ROWS, tc),
                               lambda b, i, j: (b, jnp.minimum((i + 1) * hb, nhb - 1), j + off)),
                  pl.BlockSpec((tb, kw * ext), lambda b, i, j: (0, 0)),
                  pl.BlockSpec((kw, BF16_ROWS, tc), lambda b, i, j: (0, 0, j)),
                  pl.BlockSpec((1, tc), lambda b, i, j: (0, j))],
        out_specs=pl.BlockSpec((None, tl, tc), lambda b, i, j: (b, i, j)),
        scratch_shapes=[pltpu.VMEM((kw * ext, tc), BF16)],
        compiler_params=_cp("parallel", "parallel", "parallel"),
        name="ssm_conv_silu",
    )(zx, zx, zx, _shift_matrix(tb, kw, ext, BF16_ROWS, None),
      jnp.broadcast_to(conv_w.astype(BF16)[:, None, :], (kw, BF16_ROWS, ch)),
      conv_b.reshape(1, ch))


def _split2(v):
    hi = v.astype(BF16)
    return hi, (v - hi.astype(F32)).astype(BF16)


def _ssd_kernel(*refs, reverse, gated, nsub, ncast):
    n_in = 10 if gated else 7
    cast_in = refs[n_in:n_in + ncast]
    cast_out = refs[n_in + ncast + 2:n_in + 2 * ncast + 2]
    refs = refs[:n_in] + refs[n_in + ncast:n_in + ncast + 2] + refs[n_in + 2 * ncast + 2:]
    if gated:
        (xbc_ref, dt_ref, alog_ref, dtb_ref, dsk_ref, e2_ref, h0_ref, yf_ref, z_ref, ng_ref,
         y_ref, hf_ref, s_ref, ybuf_ref) = refs
    else:
        (xbc_ref, dt_ref, alog_ref, dtb_ref, dsk_ref, e2_ref, h0_ref,
         y_ref, hf_ref, s_ref) = refs
        ybuf_ref = y_ref
    for w_ref, wb_ref in zip(cast_in, cast_out):
        wb_ref[...] = w_ref[...].astype(BF16)
    c = pl.program_id(1)
    q = SSD_CHUNK
    hp = SSM_HEAD_DIM
    gw = (SSM_HEADS // SSM_GROUPS) * hp
    lane0 = SSM_HEADS if reverse else 0
    end = 0 if reverse else q - 1

    @pl.when(c == 0)
    def _():
        s_ref[...] = h0_ref[...]

    ri = lax.broadcasted_iota(jnp.int32, (q, q), 0)
    ci = lax.broadcasted_iota(jnp.int32, (q, q), 1)
    tri = (ci >= ri) if reverse else (ci <= ri)
    tmat = jnp.where(tri, 1.0, 0.0).astype(BF16)
    e2 = e2_ref[...]
    lane = lax.broadcasted_iota(jnp.int32, (q, 2 * hp), 1)

    def expand(v):
        hi, lo = _split2(v)
        return _dot(jnp.concatenate([hi, lo], axis=1), e2)

    for sub in (reversed(range(nsub)) if reverse else range(nsub)):
        rows = slice(sub * q, (sub + 1) * q)
        dt = _softplus(dt_ref[rows, :] + dtb_ref[...])
        dta = dt * (-jnp.exp(alog_ref[...]))
        p0 = dta.astype(BF16)
        r0 = dta - p0.astype(F32)
        p1 = r0.astype(BF16)
        p2 = (r0 - p1.astype(F32)).astype(BF16)
        cs = _dot(tmat, p0) + _dot(tmat, p1) + _dot(tmat, p2)
        cs_t = cs.T
        cs_end = cs[end:end + 1, :]
        dt_x = expand(dt)
        ecs_x = expand(jnp.exp(cs))
        dte_x = expand(jnp.exp(cs_end - cs))

        for g in range(SSM_GROUPS):
            lo, hi = g * gw, (g + 1) * gw
            bg = xbc_ref[rows, D_INNER + g * SSM_STATE:D_INNER + (g + 1) * SSM_STATE]
            cg = xbc_ref[rows, D_INNER + GN + g * SSM_STATE:D_INNER + GN + (g + 1) * SSM_STATE]
            scores = lax.dot_general(cg, bg, (((1,), (1,)), ((), ())),
                                     preferred_element_type=F32)
            xs_g = xbc_ref[rows, lo:hi].astype(F32)
            xdt_g = xs_g * dt_x[:, lo:hi]
            xdt_b = xdt_g.astype(BF16)
            s_g = s_ref[:, lo:hi]
            y_g = _dot(cg, s_g.astype(BF16)) * ecs_x[:, lo:hi] + dsk_ref[:, lo:hi] * xs_g
            s_ref[:, lo:hi] = ecs_x[end:end + 1, lo:hi] * s_g + lax.dot_general(
                bg, (xdt_g * dte_x[:, lo:hi]).astype(BF16), (((0,), (0,)), ((), ())),
                preferred_element_type=F32)
            for kp in range(gw // (2 * hp)):
                gmats = []
                for hh in range(2):
                    hd = lane0 + g * (gw // hp) + 2 * kp + hh
                    seg = cs[:, hd:hd + 1] - cs_t[hd:hd + 1, :]
                    decay = jnp.exp(jnp.where(tri, seg, -jnp.inf))
                    gmats.append((scores * decay).astype(BF16))
                xp = xdt_b[:, kp * 2 * hp:(kp + 1) * 2 * hp]
                rhs = jnp.concatenate([jnp.where(lane < hp, xp, jnp.zeros_like(xp)),
                                       jnp.where(lane >= hp, xp, jnp.zeros_like(xp))], axis=0)
                yd = _dot(jnp.concatenate(gmats, axis=1), rhs)
                cols = slice(lo + kp * 2 * hp, lo + (kp + 1) * 2 * hp)
                y_pair = yd + y_g[:, kp * 2 * hp:(kp + 1) * 2 * hp]
                if gated:
                    zz = z_ref[rows, cols].astype(F32)
                    y_pair = (y_pair + yf_ref[rows, cols]) * (zz * _sigmoid(zz))
                ybuf_ref[rows, cols] = y_pair

    @pl.when(c == pl.num_programs(1) - 1)
    def _():
        hf_ref[...] = s_ref[...]

    if gated:
        load, _ = _row_access(ybuf_ref, False, 0)
        _norm_mod_rows(load, y_ref, 0, nsub * q, ng_ref[...], 0.0)


def _ssd(xa, dt, a_log, dt_bias, d_skip, e2, h0, reverse, gate_args=None, cast=()):
    bsz, l, _ = xa.shape
    nsub = SSD_CHUNKS_PER_STEP
    rows = nsub * SSD_CHUNK
    nc = l // rows
    gated = gate_args is not None
    cmap = (lambda b, c: (b, nc - 1 - c, 0)) if reverse else (lambda b, c: (b, c, 0))
    const = lambda b, c: (0, 0)
    state = pl.BlockSpec((None, SSM_STATE, D_INNER), lambda b, c: (b, 0, 0))
    tok = pl.BlockSpec((None, rows, D_INNER), cmap)
    in_specs = [pl.BlockSpec((None, rows, XBC), cmap),
                pl.BlockSpec((None, rows, 2 * SSM_HEADS), cmap),
                pl.BlockSpec((1, 2 * SSM_HEADS), const),
                pl.BlockSpec((1, 2 * SSM_HEADS), const),
                pl.BlockSpec((1, D_INNER), const),
                pl.BlockSpec((4 * SSM_HEADS, D_INNER), const),
                state]
    args = [xa, dt, a_log, dt_bias, d_skip, e2, h0]
    scratch = [pltpu.VMEM((SSM_STATE, D_INNER), F32)]
    if gated:
        y_fwd, zx, norm_g = gate_args
        in_specs += [tok, tok, pl.BlockSpec((1, D_INNER), const)]
        args += [y_fwd, zx, norm_g.reshape(1, D_INNER)]
        scratch.append(pltpu.VMEM((rows, D_INNER), F32))
    out_shape = [jax.ShapeDtypeStruct((bsz, l, D_INNER), BF16 if gated else F32),
                 jax.ShapeDtypeStruct((bsz, SSM_STATE, D_INNER), F32)]
    out_specs = [tok, state]
    for w, p in cast:
        assert w.shape[1] == bsz * nc
        blk = (2, None) + w.shape[2:]
        in_specs.append(pl.BlockSpec(blk, lambda b, c, p=p: (p, b * nc + c, 0, 0)))
        out_specs.append(pl.BlockSpec(blk, lambda b, c: (0, b * nc + c, 0, 0)))
        out_shape.append(jax.ShapeDtypeStruct((2,) + w.shape[1:], BF16))
        args.append(w)
    return pl.pallas_call(
        functools.partial(_ssd_kernel, reverse=reverse, gated=gated, nsub=nsub, ncast=len(cast)),
        out_shape=tuple(out_shape),
        grid=(bsz, nc),
        in_specs=in_specs,
        out_specs=tuple(out_specs),
        scratch_shapes=scratch,
        compiler_params=_cp("parallel", "arbitrary"),
        name="ssd_bwd_gate_norm" if gated else "ssd_fwd",
    )(*args)


def _conf_conv_kernel(v_ref, s_ref, w_ref, b_ref, lg_ref, lb_ref, o_ref, xw_ref, cv_ref,
                      *, tl, tb, kw, lc):
    j = pl.program_id(2)
    rc = 32
    for blk in range(tl // tb):
        for r in range(0, tb, rc):
            _scale_rows_by_taps(v_ref[blk * tb + r:blk * tb + r + rc, :],
                                w_ref, xw_ref, r, tb, kw)
        cv_ref[j, blk * tb:(blk + 1) * tb, :] = _dot(s_ref[...], xw_ref[...]) + b_ref[...]

    @pl.when(j == pl.num_programs(2) - 1)
    def _():
        nj = cv_ref.shape[0]
        inv_d = 1.0 / (nj * lc)
        groups = min(MAX_ROW_GROUPS, tl // F32_ROWS)
        step = F32_ROWS * groups

        def body(i, carry):
            r0 = pl.multiple_of(i * step, step)
            stats = []
            for s in range(groups):
                rows = pl.ds(r0 + F32_ROWS * s, F32_ROWS)
                tot = cv_ref[0, rows, :]
                for c in range(1, nj):
                    tot = tot + cv_ref[c, rows, :]
                mu = jnp.sum(tot, axis=-1, keepdims=True) * inv_d
                sq = (cv_ref[0, rows, :] - mu) * (cv_ref[0, rows, :] - mu)
                for c in range(1, nj):
                    sq = sq + (cv_ref[c, rows, :] - mu) * (cv_ref[c, rows, :] - mu)
                stats.append((mu, lax.rsqrt(jnp.sum(sq, axis=-1, keepdims=True) * inv_d + EPS)))
            for s in range(0, groups, 2):
                for c in range(nj):
                    halves = []
                    for t in (s, s + 1):
                        mu, rstd = stats[t]
                        xc = cv_ref[c, pl.ds(r0 + F32_ROWS * t, F32_ROWS), :] - mu
                        halves.append(xc * rstd * lg_ref[:, c * lc:(c + 1) * lc]
                                      + lb_ref[:, c * lc:(c + 1) * lc])
                    y = jnp.concatenate(halves, axis=0)
                    o_ref[pl.ds(r0 + F32_ROWS * s, BF16_ROWS), c * lc:(c + 1) * lc] = (
                        y * _sigmoid(y)).astype(BF16)
            return carry

        lax.fori_loop(0, tl // step, body, 0)


def _conf_conv(v, dw_w, dw_b, ln_g, ln_b, seg, tl, tb, lc):
    bsz, l, d = v.shape
    kw = dw_w.shape[0]
    vec = pl.BlockSpec((1, d), lambda b, i, j: (0, 0))
    return pl.pallas_call(
        functools.partial(_conf_conv_kernel, tl=tl, tb=tb, kw=kw, lc=lc),
        out_shape=jax.ShapeDtypeStruct((bsz, l, d), BF16),
        grid=(bsz, l // tl, d // lc),
        in_specs=[pl.BlockSpec((None, tl, lc), lambda b, i, j: (b, i, j)),
                  pl.BlockSpec((tb, kw * tb), lambda b, i, j: (0, 0)),
                  pl.BlockSpec((kw, BF16_ROWS, lc), lambda b, i, j: (0, 0, j)),
                  pl.BlockSpec((1, lc), lambda b, i, j: (0, j)),
                  vec, vec],
        out_specs=pl.BlockSpec((None, tl, d), lambda b, i, j: (b, i, 0)),
        scratch_shapes=[pltpu.VMEM((kw * tb, lc), BF16), pltpu.VMEM((d // lc, tl, lc), F32)],
        compiler_params=_cp("parallel", "parallel", "arbitrary"),
        name="conf_conv_ln_swish",
    )(v, _shift_matrix(tb, kw, tb, 0, seg),
      jnp.broadcast_to(dw_w.astype(BF16)[:, None, :], (kw, BF16_ROWS, d)),
      dw_b.reshape(1, d), ln_g.reshape(1, d), ln_b.reshape(1, d))


def _ssm_stream(h, mods, pre_g, post_g, li, w_in, conv_w, conv_b, ssd_f, ssd_b, e2f, e2b,
                norm_g, w_out, h0f, h0b, tm, nseq=1, cast=()):
    sh1, sc1, g1 = mods
    bflat, lflat, d = h.shape
    lseq = lflat // nseq
    zx, dt = _nmm_plain(h, pre_g, sh1, sc1, w_in, li, ZX, tm, 2048)
    zx, dt = zx.reshape(bflat * nseq, lseq, ZX), dt.reshape(bflat * nseq, lseq, -1)
    xa = _ssm_conv(zx, conv_w, conv_b, min(tm, lseq), 1024)
    y_f, s_f, *casted = _ssd(xa, dt, *ssd_f, e2f, h0f, reverse=False, cast=cast)
    y, s_b = _ssd(xa, dt, *ssd_b, e2b, h0b, reverse=True, gate_args=(y_f, zx, norm_g))
    h = _post(y.reshape(bflat, lflat, D_INNER), w_out, li, jnp.zeros((d,), F32), post_g, g1, h,
              tm, 2048)
    return h, s_f, s_b, casted


def _conf_stream(h, mods, pre_g, post_g, li, pw1_w, pw1_b, dw_w, dw_b, ln_g, ln_b, pw2_w, pw2_b,
                 seg, tm):
    sh1, sc1, g1 = mods
    tb = max(seg, 128)
    v = _nmm_glu(h, pre_g, sh1, sc1, pw1_w, li, pw1_b, tm, 1024)
    u = _conf_conv(v, dw_w, dw_b, ln_g, ln_b, seg, 256, tb, 1024)
    return _post(u, pw2_w, li, pw2_b, post_g, g1, h, tm, pw2_w.shape[1])


def kernel(x, c, ctx, c_ctx, mod_w, mod_b, pre_mix_g, post_mix_g, pre_mlp_g, post_mlp_g, mlp_w1, mlp_w2, ssm_in_w, ssm_conv_w, ssm_conv_b, ssm_a_log_f, ssm_dt_bias_f, ssm_d_f, ssm_a_log_b, ssm_dt_bias_b, ssm_d_b, ssm_norm_g, ssm_out_w, conf_pw1_w, conf_pw1_b, conf_dw_w, conf_dw_b, conf_ln_g, conf_ln_b, conf_pw2_w, conf_pw2_b):
    bsz, seq_len, d = x.shape
    ctx_len = ctx.shape[1]
    depth = mod_w.shape[0]
    rows = seq_len // GRID_W
    tm, tmc = 512, bsz * ctx_len
    mods = _adaln(c, c_ctx, mod_w, mod_b)
    ff = mlp_w1.shape[2]
    ssd_steps = bsz * seq_len // (SSD_CHUNKS_PER_STEP * SSD_CHUNK)
    mlp_b = {}
    ssm_in_b, ssm_out_b = ssm_in_w.astype(BF16), ssm_out_w.astype(BF16)
    conf_pw1_b16, conf_pw2_b16 = conf_pw1_w.astype(BF16), conf_pw2_w.astype(BF16)

    head_of_chan = jnp.arange(D_INNER, dtype=jnp.int32) // SSM_HEAD_DIM
    lane_head = jnp.arange(2 * SSM_HEADS, dtype=jnp.int32)
    e_f = (lane_head[:, None] == head_of_chan[None, :]).astype(BF16)
    e_b = (lane_head[:, None] == head_of_chan[None, :] + SSM_HEADS).astype(BF16)
    e2f = jnp.concatenate([e_f, e_f], axis=0)
    e2b = jnp.concatenate([e_b, e_b], axis=0)

    def is_col_major(i):
        return ((i // 2) % 2) == 1

    h = x
    hc = ctx.reshape(1, tmc, d)
    for i in range(depth):
        last = i == depth - 1
        kind = i % 2
        j = i // 2
        col_major = is_col_major(i)
        lat = [mods[i, :bsz, k][:, None, :] for k in range(6)]
        cm = [mods[i, bsz, k][None, None, :] for k in range(6)]
        if kind == 0:
            a_log = jnp.concatenate([ssm_a_log_f[j], ssm_a_log_b[j]]).reshape(1, -1)
            dt_bias = jnp.concatenate([ssm_dt_bias_f[j], ssm_dt_bias_b[j]]).reshape(1, -1)
            ssd_f = (a_log, dt_bias, jnp.repeat(ssm_d_f[j], SSM_HEAD_DIM).reshape(1, -1))
            ssd_b = (a_log, dt_bias, jnp.repeat(ssm_d_b[j], SSM_HEAD_DIM).reshape(1, -1))
            common = (j, ssm_in_b, ssm_conv_w[j], ssm_conv_b[j], ssd_f, ssd_b, e2f, e2b,
                      ssm_norm_g[j], ssm_out_b)
            zeros = jnp.zeros((bsz, SSM_STATE, D_INNER), F32)
            hc_new, s_f, s_b, _ = _ssm_stream(hc, cm[:3], pre_mix_g[i], post_mix_g[i], *common,
                                              zeros, zeros, tmc, nseq=bsz)
            pair_cast = i % 2 == 0 and i + 1 < depth and d % ssd_steps == 0 and ff % ssd_steps == 0
            cast = ((mlp_w1.reshape(depth, ssd_steps, d // ssd_steps, ff), i // 2),
                    (mlp_w2.reshape(depth, ssd_steps, ff // ssd_steps, d), i // 2)) if pair_cast else ()
            h, _, _, casted = _ssm_stream(h, lat[:3], pre_mix_g[i], post_mix_g[i], *common,
                                          s_f, s_b, tm, cast=cast)
            if pair_cast:
                w1_pair, w2_pair = casted[0].reshape(2, d, ff), casted[1].reshape(2, ff, d)
                mlp_b[i], mlp_b[i + 1] = (w1_pair, w2_pair, 0), (w1_pair, w2_pair, 1)
        else:
            common = (j, conf_pw1_b16, conf_pw1_b[j], conf_dw_w[j], conf_dw_b[j],
                      conf_ln_g[j], conf_ln_b[j], conf_pw2_b16, conf_pw2_b[j])
            seg = rows if col_major else GRID_W
            if not last:
                hc_new = _conf_stream(hc, cm[:3], pre_mix_g[i], post_mix_g[i], *common,
                                      ctx_len, tmc)
            h = _conf_stream(h, lat[:3], pre_mix_g[i], post_mix_g[i], *common, seg, tm)
        next_col_major = False if last else is_col_major(i + 1)
        if i not in mlp_b:
            mlp_b[i] = (mlp_w1.astype(BF16), mlp_w2.astype(BF16), i)
        w1, w2, wl = mlp_b[i]
        h = _mlp(h, pre_mlp_g[i], lat[3], lat[4], w1, w2, wl, post_mlp_g[i], lat[5], tm, 1024,
                 in_t=col_major and not next_col_major, out_t=next_col_major and not col_major)
        if not last:
            hc = _mlp(hc_new, pre_mlp_g[i], cm[3], cm[4], w1, w2, wl, post_mlp_g[i], cm[5], tmc, 1024)
    return h
```

```python
import functools

import jax
import jax.numpy as jnp
from jax import lax
from jax.experimental import pallas as pl
from jax.experimental.pallas import tpu as pltpu

F32 = jnp.float32
BF16 = jnp.bfloat16

EPS = 1e-6
GRID_W = 64
SSM_HEADS = 64
SSM_HEAD_DIM = 64
SSM_GROUPS = 8
SSM_STATE = 128
SSD_CHUNK = 128
SSD_CHUNKS_PER_STEP = 2
D_INNER = SSM_HEADS * SSM_HEAD_DIM
GN = SSM_GROUPS * SSM_STATE
XBC = D_INNER + 2 * GN
ZX = D_INNER + XBC

V7X_VMEM_LIMIT = 56 * 1024 * 1024
F32_ROWS = 8
BF16_ROWS = 16
MAX_ROW_GROUPS = 16


def _cp(*sem):
    return pltpu.CompilerParams(dimension_semantics=sem, vmem_limit_bytes=V7X_VMEM_LIMIT)


def _dot(a, b):
    return jnp.dot(a, b, preferred_element_type=F32)


def _sigmoid(v):
    return 1.0 / (1.0 + jnp.exp(-v))


def _softplus(v):
    return jnp.maximum(v, 0.0) + jnp.log(1.0 + jnp.exp(-jnp.abs(v)))


def _row_access(ref, transposed, seg):
    if transposed:
        def load(r, n):
            return ref[pl.ds(r, n), seg, :]

        def store(r, v):
            ref[pl.ds(r, v.shape[0]), seg, :] = v
    else:
        def load(r, n):
            return ref[pl.ds(seg * GRID_W + r, n), :]

        def store(r, v):
            ref[pl.ds(seg * GRID_W + r, v.shape[0]), :] = v
    return load, store


def _norm_mod_rows(load, u_ref, u_row0, ntok, mul, sh):
    groups = min(MAX_ROW_GROUPS, ntok // F32_ROWS)
    step = F32_ROWS * groups

    def body(i, carry):
        r0 = pl.multiple_of(i * step, step)
        rs = []
        for s in range(groups):
            xs = load(r0 + F32_ROWS * s, F32_ROWS)
            rs.append(lax.rsqrt(jnp.mean(xs * xs, axis=-1, keepdims=True) + EPS))
        for s in range(0, groups, 2):
            ua = load(r0 + F32_ROWS * s, F32_ROWS) * rs[s] * mul + sh
            ub = load(r0 + F32_ROWS * (s + 1), F32_ROWS) * rs[s + 1] * mul + sh
            u_ref[pl.ds(u_row0 + r0 + F32_ROWS * s, BF16_ROWS), :] = (
                jnp.concatenate([ua, ub], axis=0).astype(BF16))
        return carry

    lax.fori_loop(0, ntok // step, body, 0)


def _post_rows(acc_ref, acc_row0, ntok, bias, mul, load_h, store_o):
    groups = min(MAX_ROW_GROUPS, ntok // F32_ROWS)
    step = F32_ROWS * groups

    def body(i, carry):
        r0 = pl.multiple_of(i * step, step)
        rs = []
        for s in range(groups):
            f = acc_ref[pl.ds(acc_row0 + r0 + F32_ROWS * s, F32_ROWS), :] + bias
            rs.append(lax.rsqrt(jnp.mean(f * f, axis=-1, keepdims=True) + EPS))
        for s in range(groups):
            f = acc_ref[pl.ds(acc_row0 + r0 + F32_ROWS * s, F32_ROWS), :] + bias
            store_o(r0 + F32_ROWS * s, load_h(r0 + F32_ROWS * s, F32_ROWS) + f * rs[s] * mul)
        return carry

    lax.fori_loop(0, ntok // step, body, 0)


def _adaln_kernel(c_ref, w_ref, b_ref, o_ref):
    cv = c_ref[...]
    s = cv * _sigmoid(cv)
    o_ref[0] = _dot(s.astype(BF16), w_ref[0].astype(BF16)) + b_ref[0]


def _adaln(c, c_ctx, mod_w, mod_b):
    depth, d, n = mod_w.shape
    bsz = c.shape[0]
    cc = jnp.zeros((8, d), F32).at[:bsz].set(c).at[bsz].set(c_ctx)
    tn = 1024
    out = pl.pallas_call(
        _adaln_kernel,
        out_shape=jax.ShapeDtypeStruct((depth, 8, n), F32),
        grid=(depth, n // tn),
        in_specs=[pl.BlockSpec((8, d), lambda i, j: (0, 0)),
                  pl.BlockSpec((1, d, tn), lambda i, j: (i, 0, j)),
                  pl.BlockSpec((1, 1, tn), lambda i, j: (i, 0, j))],
        out_specs=pl.BlockSpec((1, 8, tn), lambda i, j: (i, 0, j)),
        compiler_params=_cp("parallel", "parallel"),
        name="adaln",
    )(cc, mod_w, mod_b.reshape(depth, 1, n))
    return out.reshape(depth, 8, 6, d)


def _nmm_plain_kernel(x_ref, g_ref, sh_ref, sc_ref, w_ref, ws_ref, o_ref, os_ref, u_ref, *, tm):
    @pl.when(pl.program_id(2) == 0)
    def _():
        load, _ = _row_access(x_ref, False, 0)
        _norm_mod_rows(load, u_ref, 0, tm, g_ref[...] * (1.0 + sc_ref[...]), sh_ref[...])
        os_ref[...] = _dot(u_ref[...], ws_ref[...])

    o_ref[...] = _dot(u_ref[...], w_ref[...]).astype(o_ref.dtype)


def _nmm_plain(x, g, sh, sc, w, li, n, tm, tn):
    bsz, l, d = x.shape
    ns = w.shape[2] - n
    assert n % tn == 0 and n % ns == 0
    vec = pl.BlockSpec((1, d), lambda b, i, j: (0, 0))
    mod = pl.BlockSpec((None, 1, d), lambda b, i, j: (b, 0, 0))
    return pl.pallas_call(
        functools.partial(_nmm_plain_kernel, tm=tm),
        out_shape=(jax.ShapeDtypeStruct((bsz, l, n), BF16),
                   jax.ShapeDtypeStruct((bsz, l, ns), F32)),
        grid=(bsz, l // tm, n // tn),
        in_specs=[pl.BlockSpec((None, tm, d), lambda b, i, j: (b, i, 0)), vec, mod, mod,
                  pl.BlockSpec((None, d, tn), lambda b, i, j: (li, 0, j)),
                  pl.BlockSpec((None, d, ns), lambda b, i, j: (li, 0, n // ns))],
        out_specs=(pl.BlockSpec((None, tm, tn), lambda b, i, j: (b, i, j)),
                   pl.BlockSpec((None, tm, ns), lambda b, i, j: (b, i, 0))),
        scratch_shapes=[pltpu.VMEM((tm, d), BF16)],
        compiler_params=_cp("parallel", "parallel", "arbitrary"),
        name="norm_mod_matmul",
    )(x, g.reshape(1, d), sh, sc, w, w)


def _nmm_glu_kernel(x_ref, g_ref, sh_ref, sc_ref, wa_ref, wg_ref, ba_ref, bg_ref, o_ref, u_ref,
                    *, tm):
    @pl.when(pl.program_id(2) == 0)
    def _():
        load, _ = _row_access(x_ref, False, 0)
        _norm_mod_rows(load, u_ref, 0, tm, g_ref[...] * (1.0 + sc_ref[...]), sh_ref[...])

    a = _dot(u_ref[...], wa_ref[...]) + ba_ref[...]
    gt = _dot(u_ref[...], wg_ref[...]) + bg_ref[...]
    o_ref[...] = (a * _sigmoid(gt)).astype(o_ref.dtype)


def _nmm_glu(x, g, sh, sc, w, li, bias, tm, tn):
    bsz, l, d = x.shape
    n = w.shape[2] // 2
    nj = n // tn
    vec = pl.BlockSpec((1, d), lambda b, i, j: (0, 0))
    mod = pl.BlockSpec((None, 1, d), lambda b, i, j: (b, 0, 0))
    b2 = bias.reshape(1, 2 * n)
    return pl.pallas_call(
        functools.partial(_nmm_glu_kernel, tm=tm),
        out_shape=jax.ShapeDtypeStruct((bsz, l, n), BF16),
        grid=(bsz, l // tm, nj),
        in_specs=[pl.BlockSpec((None, tm, d), lambda b, i, j: (b, i, 0)), vec, mod, mod,
                  pl.BlockSpec((None, d, tn), lambda b, i, j: (li, 0, j)),
                  pl.BlockSpec((None, d, tn), lambda b, i, j: (li, 0, j + nj)),
                  pl.BlockSpec((1, tn), lambda b, i, j: (0, j)),
                  pl.BlockSpec((1, tn), lambda b, i, j: (0, j + nj))],
        out_specs=pl.BlockSpec((None, tm, tn), lambda b, i, j: (b, i, j)),
        scratch_shapes=[pltpu.VMEM((tm, d), BF16)],
        compiler_params=_cp("parallel", "parallel", "arbitrary"),
        name="norm_mod_glu",
    )(x, g.reshape(1, d), sh, sc, w, w, b2, b2)


def _post_kernel(y_ref, w_ref, b_ref, pg_ref, gate_ref, h_ref, o_ref, acc_ref, *, tm):
    k = pl.program_id(2)
    part = _dot(y_ref[...], w_ref[...])

    @pl.when(k == 0)
    def _():
        acc_ref[...] = part

    @pl.when(k > 0)
    def _():
        acc_ref[...] += part

    @pl.when(k == pl.num_programs(2) - 1)
    def _():
        load_h, _ = _row_access(h_ref, False, 0)
        _, store_o = _row_access(o_ref, False, 0)
        _post_rows(acc_ref, 0, tm, b_ref[...], pg_ref[...] * gate_ref[...], load_h, store_o)


def _post(y, w, li, bias, pg, gate, h, tm, tk):
    bsz, l, d = h.shape
    kdim = y.shape[2]
    vec = pl.BlockSpec((1, d), lambda b, i, k: (0, 0))
    mod = pl.BlockSpec((None, 1, d), lambda b, i, k: (b, 0, 0))
    hspec = pl.BlockSpec((None, tm, d), lambda b, i, k: (b, i, 0))
    return pl.pallas_call(
        functools.partial(_post_kernel, tm=tm),
        out_shape=jax.ShapeDtypeStruct(h.shape, F32),
        grid=(bsz, l // tm, kdim // tk),
        in_specs=[pl.BlockSpec((None, tm, tk), lambda b, i, k: (b, i, k)),
                  pl.BlockSpec((None, tk, d), lambda b, i, k: (li, k, 0)),
                  vec, vec, mod, hspec],
        out_specs=hspec,
        scratch_shapes=[pltpu.VMEM((tm, d), F32)],
        compiler_params=_cp("parallel", "parallel", "arbitrary"),
        name="matmul_post_norm_residual",
    )(y, w, bias.reshape(1, d), pg.reshape(1, d), gate, h)


def _mlp_kernel(x_ref, g_ref, sh_ref, sc_ref, w1_ref, w2_ref, pg_ref, gate_ref,
                o_ref, u_ref, acc_ref, *maybe_xtok_ref, tm, in_t, out_t):
    k = pl.program_id(2)
    segs = [(s, GRID_W) for s in range(tm // GRID_W)] if out_t else [(0, tm)]
    if in_t:
        xtok_ref, = maybe_xtok_ref
    else:
        xtok_ref = x_ref

    @pl.when(k == 0)
    def _():
        if in_t:
            for s in range(tm // GRID_W):
                for r in range(0, GRID_W, F32_ROWS):
                    xtok_ref[s * GRID_W + r:s * GRID_W + r + F32_ROWS, :] = (
                        x_ref[r:r + F32_ROWS, s, :])
        mul = g_ref[...] * (1.0 + sc_ref[...])
        sh = sh_ref[...]
        for s, n in segs:
            load, _ = _row_access(xtok_ref, False, s)
            _norm_mod_rows(load, u_ref, s * GRID_W, n, mul, sh)

    hid = jnp.square(jnp.maximum(_dot(u_ref[...], w1_ref[...]), 0.0)).astype(BF16)
    part = _dot(hid, w2_ref[...])

    @pl.when(k == 0)
    def _():
        acc_ref[...] = part

    @pl.when(k > 0)
    def _():
        acc_ref[...] += part

    @pl.when(k == pl.num_programs(2) - 1)
    def _():
        mul = pg_ref[...] * gate_ref[...]
        for s, n in segs:
            load_h, _ = _row_access(xtok_ref, False, s)
            _, store_o = _row_access(o_ref, out_t, s)
            _post_rows(acc_ref, s * GRID_W, n, 0.0, mul, load_h, store_o)


def _mlp(h, g, sh, sc, w1, w2, li, pg, gate, tm, tf, in_t=False, out_t=False):
    bsz, l, d = h.shape
    ff = w1.shape[2]
    rows = l // GRID_W
    vec = pl.BlockSpec((1, d), lambda b, i, k: (0, 0))
    mod = pl.BlockSpec((None, 1, d), lambda b, i, k: (b, 0, 0))
    plain = pl.BlockSpec((None, tm, d), lambda b, i, k: (b, i, 0))
    if in_t or out_t:
        assert tm % GRID_W == 0 and (tm // GRID_W) % F32_ROWS == 0 and not (in_t and out_t)
    grid_view = pl.BlockSpec((None, GRID_W, tm // GRID_W, d), lambda b, i, k: (b, 0, i, 0))
    xin = h.reshape(bsz, GRID_W, rows, d) if in_t else h
    oshape = (bsz, GRID_W, rows, d) if out_t else (bsz, l, d)
    out = pl.pallas_call(
        functools.partial(_mlp_kernel, tm=tm, in_t=in_t, out_t=out_t),
        out_shape=jax.ShapeDtypeStruct(oshape, F32),
        grid=(bsz, l // tm, ff // tf),
        in_specs=[grid_view if in_t else plain, vec, mod, mod,
                  pl.BlockSpec((None, d, tf), lambda b, i, k: (li, 0, k)),
                  pl.BlockSpec((None, tf, d), lambda b, i, k: (li, k, 0)),
                  vec, mod],
        out_specs=grid_view if out_t else plain,
        scratch_shapes=[pltpu.VMEM((tm, d), BF16), pltpu.VMEM((tm, d), F32)]
        + ([pltpu.VMEM((tm, d), F32)] if in_t else []),
        compiler_params=_cp("parallel", "parallel", "arbitrary"),
        name="mlp_sq_relu",
    )(xin, g.reshape(1, d), sh, sc, w1, w2, pg.reshape(1, d), gate)
    return out.reshape(bsz, l, d)


def _shift_matrix(tb, kw, ext, pad, seg):
    t = jnp.arange(tb, dtype=jnp.int32)
    k = jnp.arange(kw, dtype=jnp.int32)
    e = jnp.arange(ext, dtype=jnp.int32) - pad
    src = t[:, None, None] + k[None, :, None] - kw // 2
    hit = src == e[None, None, :]
    if seg is not None:
        hit = hit & (src // seg == t[:, None, None] // seg)
    return hit.reshape(tb, kw * ext).astype(BF16)


def _scale_rows_by_taps(x, w_ref, xw_ref, row0, ext, kw):
    n, ch = x.shape
    x3 = x.reshape(n // BF16_ROWS, BF16_ROWS, ch)
    for k in range(kw):
        xw_ref[k * ext + row0:k * ext + row0 + n, :] = (x3 * w_ref[k][None]).reshape(n, ch)


def _ssm_conv_kernel(cur_ref, prev_ref, next_ref, s_ref, w_ref, b_ref, o_ref, xw_ref,
                     *, tl, tb, kw):
    i = pl.program_id(1)
    halo = BF16_ROWS
    ext = tb + 2 * halo
    rc = 32
    nblk = tl // tb
    keep_prev = jnp.where(i > 0, 1.0, 0.0).astype(BF16)
    keep_next = jnp.where(i < pl.num_programs(1) - 1, 1.0, 0.0).astype(BF16)
    for blk in range(nblk):
        t0 = blk * tb
        if blk == 0:
            before = prev_ref[...] * keep_prev
        else:
            before = cur_ref[t0 - halo:t0, :]
        if blk == nblk - 1:
            after = next_ref[...] * keep_next
        else:
            after = cur_ref[t0 + tb:t0 + tb + halo, :]
        _scale_rows_by_taps(before, w_ref, xw_ref, 0, ext, kw)
        for r in range(0, tb, rc):
            _scale_rows_by_taps(cur_ref[t0 + r:t0 + r + rc, :], w_ref, xw_ref,
                                halo + r, ext, kw)
        _scale_rows_by_taps(after, w_ref, xw_ref, halo + tb, ext, kw)
        acc = _dot(s_ref[...], xw_ref[...]) + b_ref[...]
        o_ref[t0:t0 + tb, :] = (acc * _sigmoid(acc)).astype(o_ref.dtype)


def _ssm_conv(zx, conv_w, conv_b, tl, tc):
    bsz, l, _ = zx.shape
    kw, ch = conv_w.shape
    tb = 128
    ext = tb + 2 * BF16_ROWS
    off = D_INNER // tc
    hb = tl // BF16_ROWS
    nhb = l // BF16_ROWS
    return pl.pallas_call(
        functools.partial(_ssm_conv_kernel, tl=tl, tb=tb, kw=kw),
        out_shape=jax.ShapeDtypeStruct((bsz, l, ch), BF16),
        grid=(bsz, l // tl, ch // tc),
        in_specs=[pl.BlockSpec((None, tl, tc), lambda b, i, j: (b, i, j + off)),
                  pl.BlockSpec((None, BF16_ROWS, tc),
                               lambda b, i, j: (b, jnp.maximum(i * hb - 1, 0), j + off)),
                  pl.BlockSpec((None, BF16_ROWS, tc),
                               lambda b, i, j: (b, jnp.minimum((i + 1) * hb, nhb - 1), j + off)),
                  pl.BlockSpec((tb, kw * ext), lambda b, i, j: (0, 0)),
                  pl.BlockSpec((kw, BF16_ROWS, tc), lambda b, i, j: (0, 0, j)),
                  pl.BlockSpec((1, tc), lambda b, i, j: (0, j))],
        out_specs=pl.BlockSpec((None, tl, tc), lambda b, i, j: (b, i, j)),
        scratch_shapes=[pltpu.VMEM((kw * ext, tc), BF16)],
        compiler_params=_cp("parallel", "parallel", "parallel"),
        name="ssm_conv_silu",
    )(zx, zx, zx, _shift_matrix(tb, kw, ext, BF16_ROWS, None),
      jnp.broadcast_to(conv_w.astype(BF16)[:, None, :], (kw, BF16_ROWS, ch)),
      conv_b.reshape(1, ch))


def _split2(v):
    hi = v.astype(BF16)
    return hi, (v - hi.astype(F32)).astype(BF16)


def _ssd_kernel(*refs, reverse, gated, nsub, ncast):
    n_in = 10 if gated else 7
    cast_in = refs[n_in:n_in + ncast]
    cast_out = refs[n_in + ncast + 2:n_in + 2 * ncast + 2]
    refs = refs[:n_in] + refs[n_in + ncast:n_in + ncast + 2] + refs[n_in + 2 * ncast + 2:]
    if gated:
        (xbc_ref, dt_ref, alog_ref, dtb_ref, dsk_ref, e2_ref, h0_ref, yf_ref, z_ref, ng_ref,
         y_ref, hf_ref, s_ref, ybuf_ref) = refs
    else:
        (xbc_ref, dt_ref, alog_ref, dtb_ref, dsk_ref, e2_ref, h0_ref,
         y_ref, hf_ref, s_ref) = refs
        ybuf_ref = y_ref
    for w_ref, wb_ref in zip(cast_in, cast_out):
        wb_ref[...] = w_ref[...].astype(BF16)
    c = pl.program_id(1)
    q = SSD_CHUNK
    hp = SSM_HEAD_DIM
    gw = (SSM_HEADS // SSM_GROUPS) * hp
    lane0 = SSM_HEADS if reverse else 0
    end = 0 if reverse else q - 1

    @pl.when(c == 0)
    def _():
        s_ref[...] = h0_ref[...]

    ri = lax.broadcasted_iota(jnp.int32, (q, q), 0)
    ci = lax.broadcasted_iota(jnp.int32, (q, q), 1)
    tri = (ci >= ri) if reverse else (ci <= ri)
    tmat = jnp.where(tri, 1.0, 0.0).astype(BF16)
    e2 = e2_ref[...]
    lane = lax.broadcasted_iota(jnp.int32, (q, 2 * hp), 1)

    def expand(v):
        hi, lo = _split2(v)
        return _dot(jnp.concatenate([hi, lo], axis=1), e2)

    for sub in (reversed(range(nsub)) if reverse else range(nsub)):
        rows = slice(sub * q, (sub + 1) * q)
        dt = _softplus(dt_ref[rows, :] + dtb_ref[...])
        dta = dt * (-jnp.exp(alog_ref[...]))
        p0 = dta.astype(BF16)
        r0 = dta - p0.astype(F32)
        p1 = r0.astype(BF16)
        p2 = (r0 - p1.astype(F32)).astype(BF16)
        cs = _dot(tmat, p0) + _dot(tmat, p1) + _dot(tmat, p2)
        cs_t = cs.T
        cs_end = cs[end:end + 1, :]
        dt_x = expand(dt)
        ecs_x = expand(jnp.exp(cs))
        dte_x = expand(jnp.exp(cs_end - cs))

        for g in range(SSM_GROUPS):
            lo, hi = g * gw, (g + 1) * gw
            bg = xbc_ref[rows, D_INNER + g * SSM_STATE:D_INNER + (g + 1) * SSM_STATE]
            cg = xbc_ref[rows, D_INNER + GN + g * SSM_STATE:D_INNER + GN + (g + 1) * SSM_STATE]
            scores = lax.dot_general(cg, bg, (((1,), (1,)), ((), ())),
                                     preferred_element_type=F32)
            xs_g = xbc_ref[rows, lo:hi].astype(F32)
            xdt_g = xs_g * dt_x[:, lo:hi]
            xdt_b = xdt_g.astype(BF16)
            s_g = s_ref[:, lo:hi]
            y_g = _dot(cg, s_g.astype(BF16)) * ecs_x[:, lo:hi] + dsk_ref[:, lo:hi] * xs_g
            s_ref[:, lo:hi] = ecs_x[end:end + 1, lo:hi] * s_g + lax.dot_general(
                bg, (xdt_g * dte_x[:, lo:hi]).astype(BF16), (((0,), (0,)), ((), ())),
                preferred_element_type=F32)
            for kp in range(gw // (2 * hp)):
                gmats = []
                for hh in range(2):
                    hd = lane0 + g * (gw // hp) + 2 * kp + hh
                    seg = cs[:, hd:hd + 1] - cs_t[hd:hd + 1, :]
                    decay = jnp.exp(jnp.where(tri, seg, -jnp.inf))
                    gmats.append((scores * decay).astype(BF16))
                xp = xdt_b[:, kp * 2 * hp:(kp + 1) * 2 * hp]
                rhs = jnp.concatenate([jnp.where(lane < hp, xp, jnp.zeros_like(xp)),
                                       jnp.where(lane >= hp, xp, jnp.zeros_like(xp))], axis=0)
                yd = _dot(jnp.concatenate(gmats, axis=1), rhs)
                cols = slice(lo + kp * 2 * hp, lo + (kp + 1) * 2 * hp)
                y_pair = yd + y_g[:, kp * 2 * hp:(kp + 1) * 2 * hp]
                if gated:
                    zz = z_ref[rows, cols].astype(F32)
                    y_pair = (y_pair + yf_ref[rows, cols]) * (zz * _sigmoid(zz))
                ybuf_ref[rows, cols] = y_pair

    @pl.when(c == pl.num_programs(1) - 1)
    def _():
        hf_ref[...] = s_ref[...]

    if gated:
        load, _ = _row_access(ybuf_ref, False, 0)
        _norm_mod_rows(load, y_ref, 0, nsub * q, ng_ref[...], 0.0)


def _ssd(xa, dt, a_log, dt_bias, d_skip, e2, h0, reverse, gate_args=None, cast=()):
    bsz, l, _ = xa.shape
    nsub = SSD_CHUNKS_PER_STEP
    rows = nsub * SSD_CHUNK
    nc = l // rows
    gated = gate_args is not None
    cmap = (lambda b, c: (b, nc - 1 - c, 0)) if reverse else (lambda b, c: (b, c, 0))
    const = lambda b, c: (0, 0)
    state = pl.BlockSpec((None, SSM_STATE, D_INNER), lambda b, c: (b, 0, 0))
    tok = pl.BlockSpec((None, rows, D_INNER), cmap)
    in_specs = [pl.BlockSpec((None, rows, XBC), cmap),
                pl.BlockSpec((None, rows, 2 * SSM_HEADS), cmap),
                pl.BlockSpec((1, 2 * SSM_HEADS), const),
                pl.BlockSpec((1, 2 * SSM_HEADS), const),
                pl.BlockSpec((1, D_INNER), const),
                pl.BlockSpec((4 * SSM_HEADS, D_INNER), const),
                state]
    args = [xa, dt, a_log, dt_bias, d_skip, e2, h0]
    scratch = [pltpu.VMEM((SSM_STATE, D_INNER), F32)]
    if gated:
        y_fwd, zx, norm_g = gate_args
        in_specs += [tok, tok, pl.BlockSpec((1, D_INNER), const)]
        args += [y_fwd, zx, norm_g.reshape(1, D_INNER)]
        scratch.append(pltpu.VMEM((rows, D_INNER), F32))
    out_shape = [jax.ShapeDtypeStruct((bsz, l, D_INNER), BF16 if gated else F32),
                 jax.ShapeDtypeStruct((bsz, SSM_STATE, D_INNER), F32)]
    out_specs = [tok, state]
    for w, p in cast:
        assert w.shape[1] == bsz * nc
        blk = (2, None) + w.shape[2:]
        in_specs.append(pl.BlockSpec(blk, lambda b, c, p=p: (p, b * nc + c, 0, 0)))
        out_specs.append(pl.BlockSpec(blk, lambda b, c: (0, b * nc + c, 0, 0)))
        out_shape.append(jax.ShapeDtypeStruct((2,) + w.shape[1:], BF16))
        args.append(w)
    return pl.pallas_call(
        functools.partial(_ssd_kernel, reverse=reverse, gated=gated, nsub=nsub, ncast=len(cast)),
        out_shape=tuple(out_shape),
        grid=(bsz, nc),
        in_specs=in_specs,
        out_specs=tuple(out_specs),
        scratch_shapes=scratch,
        compiler_params=_cp("parallel", "arbitrary"),
        name="ssd_bwd_gate_norm" if gated else "ssd_fwd",
    )(*args)


def _conf_conv_kernel(v_ref, s_ref, w_ref, b_ref, lg_ref, lb_ref, o_ref, xw_ref, cv_ref,
                      *, tl, tb, kw, lc):
    j = pl.program_id(2)
    rc = 32
    for blk in range(tl // tb):
        for r in range(0, tb, rc):
            _scale_rows_by_taps(v_ref[blk * tb + r:blk * tb + r + rc, :],
                                w_ref, xw_ref, r, tb, kw)
        cv_ref[j, blk * tb:(blk + 1) * tb, :] = _dot(s_ref[...], xw_ref[...]) + b_ref[...]

    @pl.when(j == pl.num_programs(2) - 1)
    def _():
        nj = cv_ref.shape[0]
        inv_d = 1.0 / (nj * lc)
        groups = min(MAX_ROW_GROUPS, tl // F32_ROWS)
        step = F32_ROWS * groups

        def body(i, carry):
            r0 = pl.multiple_of(i * step, step)
            stats = []
            for s in range(groups):
                rows = pl.ds(r0 + F32_ROWS * s, F32_ROWS)
                tot = cv_ref[0, rows, :]
                for c in range(1, nj):
                    tot = tot + cv_ref[c, rows, :]
                mu = jnp.sum(tot, axis=-1, keepdims=True) * inv_d
                sq = (cv_ref[0, rows, :] - mu) * (cv_ref[0, rows, :] - mu)
                for c in range(1, nj):
                    sq = sq + (cv_ref[c, rows, :] - mu) * (cv_ref[c, rows, :] - mu)
                stats.append((mu, lax.rsqrt(jnp.sum(sq, axis=-1, keepdims=True) * inv_d + EPS)))
            for s in range(0, groups, 2):
                for c in range(nj):
                    halves = []
                    for t in (s, s + 1):
                        mu, rstd = stats[t]
                        xc = cv_ref[c, pl.ds(r0 + F32_ROWS * t, F32_ROWS), :] - mu
                        halves.append(xc * rstd * lg_ref[:, c * lc:(c + 1) * lc]
                                      + lb_ref[:, c * lc:(c + 1) * lc])
                    y = jnp.concatenate(halves, axis=0)
                    o_ref[pl.ds(r0 + F32_ROWS * s, BF16_ROWS), c * lc:(c + 1) * lc] = (
                        y * _sigmoid(y)).astype(BF16)
            return carry

        lax.fori_loop(0, tl // step, body, 0)


def _conf_conv(v, dw_w, dw_b, ln_g, ln_b, seg, tl, tb, lc):
    bsz, l, d = v.shape
    kw = dw_w.shape[0]
    vec = pl.BlockSpec((1, d), lambda b, i, j: (0, 0))
    return pl.pallas_call(
        functools.partial(_conf_conv_kernel, tl=tl, tb=tb, kw=kw, lc=lc),
        out_shape=jax.ShapeDtypeStruct((bsz, l, d), BF16),
        grid=(bsz, l // tl, d // lc),
        in_specs=[pl.BlockSpec((None, tl, lc), lambda b, i, j: (b, i, j)),
                  pl.BlockSpec((tb, kw * tb), lambda b, i, j: (0, 0)),
                  pl.BlockSpec((kw, BF16_ROWS, lc), lambda b, i, j: (0, 0, j)),
                  pl.BlockSpec((1, lc), lambda b, i, j: (0, j)),
                  vec, vec],
        out_specs=pl.BlockSpec((None, tl, d), lambda b, i, j: (b, i, 0)),
        scratch_shapes=[pltpu.VMEM((kw * tb, lc), BF16), pltpu.VMEM((d // lc, tl, lc), F32)],
        compiler_params=_cp("parallel", "parallel", "arbitrary"),
        name="conf_conv_ln_swish",
    )(v, _shift_matrix(tb, kw, tb, 0, seg),
      jnp.broadcast_to(dw_w.astype(BF16)[:, None, :], (kw, BF16_ROWS, d)),
      dw_b.reshape(1, d), ln_g.reshape(1, d), ln_b.reshape(1, d))


def _ssm_stream(h, mods, pre_g, post_g, li, w_in, conv_w, conv_b, ssd_f, ssd_b, e2f, e2b,
                norm_g, w_out, h0f, h0b, tm, nseq=1, cast=()):
    sh1, sc1, g1 = mods
    bflat, lflat, d = h.shape
    lseq = lflat // nseq
    zx, dt = _nmm_plain(h, pre_g, sh1, sc1, w_in, li, ZX, tm, 2048)
    zx, dt = zx.reshape(bflat * nseq, lseq, ZX), dt.reshape(bflat * nseq, lseq, -1)
    xa = _ssm_conv(zx, conv_w, conv_b, min(tm, lseq), 1024)
    y_f, s_f, *casted = _ssd(xa, dt, *ssd_f, e2f, h0f, reverse=False, cast=cast)
    y, s_b = _ssd(xa, dt, *ssd_b, e2b, h0b, reverse=True, gate_args=(y_f, zx, norm_g))
    h = _post(y.reshape(bflat, lflat, D_INNER), w_out, li, jnp.zeros((d,), F32), post_g, g1, h,
              tm, 2048)
    return h, s_f, s_b, casted


def _conf_stream(h, mods, pre_g, post_g, li, pw1_w, pw1_b, dw_w, dw_b, ln_g, ln_b, pw2_w, pw2_b,
                 seg, tm):
    sh1, sc1, g1 = mods
    tb = max(seg, 128)
    v = _nmm_glu(h, pre_g, sh1, sc1, pw1_w, li, pw1_b, tm, 1024)
    u = _conf_conv(v, dw_w, dw_b, ln_g, ln_b, seg, 256, tb, 1024)
    return _post(u, pw2_w, li, pw2_b, post_g, g1, h, tm, pw2_w.shape[1])


def kernel(x, c, ctx, c_ctx, mod_w, mod_b, pre_mix_g, post_mix_g, pre_mlp_g, post_mlp_g, mlp_w1, mlp_w2, ssm_in_w, ssm_conv_w, ssm_conv_b, ssm_a_log_f, ssm_dt_bias_f, ssm_d_f, ssm_a_log_b, ssm_dt_bias_b, ssm_d_b, ssm_norm_g, ssm_out_w, conf_pw1_w, conf_pw1_b, conf_dw_w, conf_dw_b, conf_ln_g, conf_ln_b, conf_pw2_w, conf_pw2_b):
    bsz, seq_len, d = x.shape
    ctx_len = ctx.shape[1]
    depth = mod_w.shape[0]
    rows = seq_len // GRID_W
    tm, tmc = 512, bsz * ctx_len
    mods = _adaln(c, c_ctx, mod_w, mod_b)
    ff = mlp_w1.shape[2]
    ssd_steps = bsz * seq_len // (SSD_CHUNKS_PER_STEP * SSD_CHUNK)
    mlp_b = {}
    ssm_in_b, ssm_out_b = ssm_in_w.astype(BF16), ssm_out_w.astype(BF16)
    conf_pw1_b16, conf_pw2_b16 = conf_pw1_w.astype(BF16), conf_pw2_w.astype(BF16)

    head_of_chan = jnp.arange(D_INNER, dtype=jnp.int32) // SSM_HEAD_DIM
    lane_head = jnp.arange(2 * SSM_HEADS, dtype=jnp.int32)
    e_f = (lane_head[:, None] == head_of_chan[None, :]).astype(BF16)
    e_b = (lane_head[:, None] == head_of_chan[None, :] + SSM_HEADS).astype(BF16)
    e2f = jnp.concatenate([e_f, e_f], axis=0)
    e2b = jnp.concatenate([e_b, e_b], axis=0)

    def is_col_major(i):
        return ((i // 2) % 2) == 1

    h = x
    hc = ctx.reshape(1, tmc, d)
    for i in range(depth):
        last = i == depth - 1
        kind = i % 2
        j = i // 2
        col_major = is_col_major(i)
        lat = [mods[i, :bsz, k][:, None, :] for k in range(6)]
        cm = [mods[i, bsz, k][None, None, :] for k in range(6)]
        if kind == 0:
            a_log = jnp.concatenate([ssm_a_log_f[j], ssm_a_log_b[j]]).reshape(1, -1)
            dt_bias = jnp.concatenate([ssm_dt_bias_f[j], ssm_dt_bias_b[j]]).reshape(1, -1)
            ssd_f = (a_log, dt_bias, jnp.repeat(ssm_d_f[j], SSM_HEAD_DIM).reshape(1, -1))
            ssd_b = (a_log, dt_bias, jnp.repeat(ssm_d_b[j], SSM_HEAD_DIM).reshape(1, -1))
            common = (j, ssm_in_b, ssm_conv_w[j], ssm_conv_b[j], ssd_f, ssd_b, e2f, e2b,
                      ssm_norm_g[j], ssm_out_b)
            zeros = jnp.zeros((bsz, SSM_STATE, D_INNER), F32)
            hc_new, s_f, s_b, _ = _ssm_stream(hc, cm[:3], pre_mix_g[i], post_mix_g[i], *common,
                                              zeros, zeros, tmc, nseq=bsz)
            pair_cast = i % 2 == 0 and i + 1 < depth and d % ssd_steps == 0 and ff % ssd_steps == 0
            cast = ((mlp_w1.reshape(depth, ssd_steps, d // ssd_steps, ff), i // 2),
                    (mlp_w2.reshape(depth, ssd_steps, ff // ssd_steps, d), i // 2)) if pair_cast else ()
            conf_cast = (i == 0 and pair_cast and conf_pw1_w.shape[0] == 2
                         and conf_pw1_w.shape[1] % ssd_steps == 0
                         and conf_pw2_w.shape[1] % ssd_steps == 0)
            if conf_cast:
                cast += tuple((w.reshape(2, ssd_steps, w.shape[1] // ssd_steps, w.shape[2]), 0)
                              for w in (conf_pw1_w, conf_pw2_w))
            h, _, _, casted = _ssm_stream(h, lat[:3], pre_mix_g[i], post_mix_g[i], *common,
                                          s_f, s_b, tm, cast=cast)
            if pair_cast:
                w1_pair, w2_pair = casted[0].reshape(2, d, ff), casted[1].reshape(2, ff, d)
                mlp_b[i], mlp_b[i + 1] = (w1_pair, w2_pair, 0), (w1_pair, w2_pair, 1)
            if conf_cast:
                conf_pw1_b16 = casted[2].reshape(conf_pw1_w.shape)
                conf_pw2_b16 = casted[3].reshape(conf_pw2_w.shape)
        else:
            common = (j, conf_pw1_b16, conf_pw1_b[j], conf_dw_w[j], conf_dw_b[j],
                      conf_ln_g[j], conf_ln_b[j], conf_pw2_b16, conf_pw2_b[j])
            seg = rows if col_major else GRID_W
            if not last:
                hc_new = _conf_stream(hc, cm[:3], pre_mix_g[i], post_mix_g[i], *common,
                                      ctx_len, tmc)
            h = _conf_stream(h, lat[:3], pre_mix_g[i], post_mix_g[i], *common, seg, tm)
        next_col_major = False if last else is_col_major(i + 1)
        if i not in mlp_b:
            mlp_b[i] = (mlp_w1.astype(BF16), mlp_w2.astype(BF16), i)
        w1, w2, wl = mlp_b[i]
        h = _mlp(h, pre_mlp_g[i], lat[3], lat[4], w1, w2, wl, post_mlp_g[i], lat[5], tm, 1024,
                 in_t=col_major and not next_col_major, out_t=next_col_major and not col_major)
        if not last:
            hc = _mlp(hc_new, pre_mlp_g[i], cm[3], cm[4], w1, w2, wl, post_mlp_g[i], cm[5], tmc, 1024)
    return h
```
